```python
import math
import jax, jax.numpy as jnp
from jax import lax
import numpy as np

D_MODEL = 4096
BATCH = 1
SEQ = 16384
DEPTH = 1

CTX_LEN = 256
GRID_W = 64
ML_HEADS = 8
ML_DH = 256
ML_WIDTH = ML_HEADS * ML_DH
GD_HEADS = 16
GD_DH = 128
GD_WIDTH = GD_HEADS * GD_DH
MIX_WIDTH = ML_WIDTH + GD_WIDTH
N_DIR = 2
N_GATE = 2 * N_DIR * ML_HEADS + 2 * N_DIR * GD_HEADS
PROJ_WIDTH = 4 * ML_WIDTH + 4 * GD_WIDTH + N_GATE
CHUNK = 64
CONV_K = 5
PEER_HEADS = 8
PEER_NKEYS = 128
PEER_DKEY = 256
PEER_HALF = PEER_DKEY // 2
PEER_TOPK = 16
N_EXPERTS = PEER_NKEYS * PEER_NKEYS
PEER_BLOCK = 32
EPS = 1e-6
NEG = -1e30

kernel_name = 'hybrid_mlstm_gdn_peer_dit_block'


def _rmsnorm(x, g):
    xf = x.astype(jnp.float32)
    y = xf * lax.rsqrt(jnp.mean(xf * xf, axis=-1, keepdims=True) + EPS)
    return (y * g.astype(jnp.float32)).astype(x.dtype)


def _modulate(h, shift, scale):
    return h * (1 + scale) + shift


def _heads(t, n_heads):
    b, t_len, w = t.shape
    return t.reshape(b, t_len, n_heads, w // n_heads).transpose(0, 2, 1, 3)


def _merge_heads(t):
    b, h, t_len, d = t.shape
    return t.transpose(0, 2, 1, 3).reshape(b, t_len, h * d)


def _gates(t, n_heads):
    b, t_len, _ = t.shape
    return t.reshape(b, t_len, N_DIR, n_heads).transpose(2, 0, 3, 1)


def _head_rms(t, g):
    y = t * lax.rsqrt(jnp.mean(t * t, axis=-1, keepdims=True) + EPS)
    return y * g.astype(jnp.float32)[None, :, None, :]


def _l2norm(t):
    return t * lax.rsqrt(jnp.sum(t * t, axis=-1, keepdims=True) + EPS)


def _grid_to_col(t, rows):
    b, t_len, ch = t.shape
    return t.reshape(b, rows, GRID_W, ch).transpose(0, 2, 1, 3).reshape(b, t_len, ch)


def _col_to_grid(t, rows):
    b, t_len, ch = t.shape
    return t.reshape(b, GRID_W, rows, ch).transpose(0, 2, 1, 3).reshape(b, t_len, ch)


def _flip_if(t, rev):
    return jnp.flip(t, axis=2) if rev else t


def _short_conv(t, w, b):
    ch = t.shape[-1]
    y = lax.conv_general_dilated(t, w[:, None, :], window_strides=(1,),
                                 padding=[(CONV_K // 2, CONV_K // 2)],
                                 dimension_numbers=('NWC', 'WIO', 'NWC'),
                                 feature_group_count=ch)
    return y + b


def _split_proj(p, b_gate):
    main = 4 * ML_WIDTH + 4 * GD_WIDTH
    sizes = [ML_WIDTH] * 4 + [GD_WIDTH] * 4
    idx = []
    acc = 0
    for s in sizes[:-1]:
        acc += s
        idx.append(acc)
    pieces = jnp.split(p[..., :main], idx, axis=-1)
    gates = p[..., main:] + b_gate
    gidx = [N_DIR * ML_HEADS, 2 * N_DIR * ML_HEADS, 2 * N_DIR * ML_HEADS + N_DIR * GD_HEADS]
    gpieces = jnp.split(gates, gidx, axis=-1)
    return list(pieces) + list(gpieces)


def _mlstm_inputs(mq, mk, mv, mo, mi, mf):
    q = _heads(mq, ML_HEADS).astype(jnp.float32) * (ML_DH ** -0.5)
    k = _heads(mk, ML_HEADS).astype(jnp.float32)
    v = _heads(mv, ML_HEADS).astype(jnp.float32)
    ig = _gates(mi, ML_HEADS).astype(jnp.float32)
    lf = jax.nn.log_sigmoid(_gates(mf, ML_HEADS).astype(jnp.float32))
    o = jax.nn.sigmoid(mo)
    return q, k, v, ig, lf, o


def _mlstm_scan(q, k, v, ig, lf, state):
    b_, h_, t_len, dh = q.shape
    nc = t_len // CHUNK

    def chunks(t):
        return jnp.moveaxis(t.reshape(b_, h_, nc, CHUNK, *t.shape[3:]), 2, 0)

    tril = jnp.tril(jnp.ones((CHUNK, CHUNK), dtype=bool))

    def body(carry, xs):
        c_st, n_st, m_st = carry
        qc, kc, vc, ic, fc = xs
        bcum = jnp.cumsum(fc, axis=-1)
        dmat = jnp.where(tril, bcum[..., :, None] - bcum[..., None, :] + ic[..., None, :], NEG)
        m_inter = bcum + m_st[..., None]
        m_row = jnp.maximum(m_inter, jnp.max(dmat, axis=-1))
        s = jnp.einsum('bhtd,bhsd->bhts', qc, kc) * jnp.exp(dmat - m_row[..., None])
        w_inter = jnp.exp(m_inter - m_row)
        num = w_inter[..., None] * jnp.einsum('bhtd,bhde->bhte', qc, c_st) + jnp.einsum('bhts,bhse->bhte', s, vc)
        den = w_inter * jnp.einsum('bhtd,bhd->bht', qc, n_st) + jnp.sum(s, axis=-1)
        h = num / jnp.maximum(jnp.abs(den), jnp.exp(-m_row))[..., None]
        b_last = bcum[..., -1]
        w_log = b_last[..., None] - bcum + ic
        m_new = jnp.maximum(b_last + m_st, jnp.max(w_log, axis=-1))
        a_st = jnp.exp(b_last + m_st - m_new)
        w_tok = jnp.exp(w_log - m_new[..., None])
        c_new = a_st[..., None, None] * c_st + jnp.einsum('bhsd,bhse->bhde', kc * w_tok[..., None], vc)
        n_new = a_st[..., None] * n_st + jnp.einsum('bhs,bhsd->bhd', w_tok, kc)
        return (c_new, n_new, m_new), h

    state, h = lax.scan(body, state, tuple(chunks(t) for t in (q, k, v, ig, lf)))
    return jnp.moveaxis(h, 0, 2).reshape(b_, h_, t_len, dh), state


def _gdn_inputs(gq, gk, gv, gz, gb, ga, conv_w, conv_b, a_log):
    qkv = jax.nn.silu(_short_conv(jnp.concatenate([gq, gk, gv], axis=-1), conv_w, conv_b))
    q, k, v = jnp.split(qkv, 3, axis=-1)
    q = _l2norm(_heads(q, GD_HEADS).astype(jnp.float32)) * (GD_DH ** -0.5)
    k = _l2norm(_heads(k, GD_HEADS).astype(jnp.float32))
    v = _heads(v, GD_HEADS).astype(jnp.float32)
    beta = jax.nn.sigmoid(_gates(gb, GD_HEADS).astype(jnp.float32))
    g = -jnp.exp(a_log.astype(jnp.float32))[:, None, :, None] * jax.nn.softplus(_gates(ga, GD_HEADS).astype(jnp.float32))
    return q, k, v, beta, g, gz


def _gdn_scan(q, k, v, beta, g, state):
    b_, h_, t_len, dk = q.shape
    dv = v.shape[-1]
    nc = t_len // CHUNK

    def chunks(t):
        return t.reshape(b_, h_, nc, CHUNK, *t.shape[3:])

    q, k, v, beta, g = (chunks(t) for t in (q, k, v, beta, g))
    g = jnp.cumsum(g, axis=-1)
    tril = jnp.tril(jnp.ones((CHUNK, CHUNK), dtype=bool))
    strict = jnp.tril(jnp.ones((CHUNK, CHUNK), dtype=bool), -1)
    decay = jnp.exp(jnp.where(tril, g[..., :, None] - g[..., None, :], NEG))
    kb = k * beta[..., None]
    lower = jnp.where(strict, jnp.einsum('bhnid,bhnjd->bhnij', kb, k) * decay, 0.0)
    rhs = jnp.concatenate([kb * jnp.exp(g)[..., None], v * beta[..., None]], axis=-1)
    sol = lax.linalg.triangular_solve(lower, rhs, left_side=True, lower=True, unit_diagonal=True)
    w, u = sol[..., :dk], sol[..., dk:]
    attn = jnp.einsum('bhnid,bhnjd->bhnij', q, k) * decay
    qg = q * jnp.exp(g)[..., None]
    g_last = g[..., -1]
    kd = k * jnp.exp(g_last[..., None] - g)[..., None]

    def body(s_st, xs):
        qg_c, w_c, u_c, attn_c, kd_c, gl_c = xs
        v_new = u_c - jnp.einsum('bhld,bhde->bhle', w_c, s_st)
        o = jnp.einsum('bhld,bhde->bhle', qg_c, s_st) + jnp.einsum('bhls,bhse->bhle', attn_c, v_new)
        s_new = jnp.exp(gl_c)[..., None, None] * s_st + jnp.einsum('bhld,bhle->bhde', kd_c, v_new)
        return s_new, o

    xs = tuple(jnp.moveaxis(t, 2, 0) for t in (qg, w, u, attn, kd, g_last))
    state, o = lax.scan(body, state, xs)
    return jnp.moveaxis(o, 0, 2).reshape(b_, h_, t_len, dv), state


def _bidir(scan_fn, state0, lat_in, ctx_in):
    out_lat = 0.0
    out_ctx = 0.0
    for d in range(N_DIR):
        rev = d == 1
        q, k, v, ga, gb = ctx_in
        h_c, st = scan_fn(_flip_if(q, rev), _flip_if(k, rev), _flip_if(v, rev),
                          _flip_if(ga[d], rev), _flip_if(gb[d], rev), state0)
        q, k, v, ga, gb = lat_in
        h_l, _ = scan_fn(_flip_if(q, rev), _flip_if(k, rev), _flip_if(v, rev),
                         _flip_if(ga[d], rev), _flip_if(gb[d], rev), st)
        out_lat = out_lat + _flip_if(h_l, rev)
        out_ctx = out_ctx + _flip_if(h_c, rev)
    return out_lat, out_ctx


def _ml_out(h, o, g):
    return (o.astype(jnp.float32) * _merge_heads(_head_rms(h, g))).astype(o.dtype)


def _gd_out(h, z, g):
    return (jax.nn.silu(z.astype(jnp.float32)) * _merge_heads(_head_rms(h, g))).astype(z.dtype)


def _mixer(h_lat, h_ctx, need_ctx, w_in, b_gate, conv_w, conv_b, a_log, ml_g, gd_g, w_out):
    b_ = h_lat.shape[0]
    rows = h_lat.shape[1] // GRID_W
    lat = _split_proj(h_lat @ w_in, b_gate)
    ctx = _split_proj(h_ctx @ w_in, b_gate)
    ml_lat = _mlstm_inputs(*lat[0:4], lat[8], lat[9])
    ml_ctx = _mlstm_inputs(*ctx[0:4], ctx[8], ctx[9])
    ml_state0 = (jnp.zeros((b_, ML_HEADS, ML_DH, ML_DH), jnp.float32),
                 jnp.zeros((b_, ML_HEADS, ML_DH), jnp.float32),
                 jnp.full((b_, ML_HEADS), NEG, jnp.float32))
    hm_lat, hm_ctx = _bidir(_mlstm_scan, ml_state0, ml_lat[:5], ml_ctx[:5])
    gd_lat_raw = [_grid_to_col(t, rows) for t in lat[4:8] + lat[10:12]]
    gd_lat = _gdn_inputs(*gd_lat_raw, conv_w, conv_b, a_log)
    gd_ctx = _gdn_inputs(*ctx[4:8], *ctx[10:12], conv_w, conv_b, a_log)
    gd_state0 = jnp.zeros((b_, GD_HEADS, GD_DH, GD_DH), jnp.float32)
    hg_lat, hg_ctx = _bidir(_gdn_scan, gd_state0, gd_lat[:5], gd_ctx[:5])
    y_lat = jnp.concatenate([_ml_out(hm_lat, ml_lat[5], ml_g),
                             _col_to_grid(_gd_out(hg_lat, gd_lat[5], gd_g), rows)], axis=-1) @ w_out
    y_ctx = None
    if need_ctx:
        y_ctx = jnp.concatenate([_ml_out(hm_ctx, ml_ctx[5], ml_g),
                                 _gd_out(hg_ctx, gd_ctx[5], gd_g)], axis=-1) @ w_out
    return y_lat, y_ctx


def _peer(h, wq, sub_keys, exp_u, exp_v):
    b_, t_len, d = h.shape
    hb = h.reshape(-1, PEER_BLOCK, d)

    def block(xb):
        q = (xb @ wq).astype(jnp.float32).reshape(PEER_BLOCK, PEER_HEADS, 2, PEER_HALF)
        s = jnp.einsum('thpc,hpkc->thpk', q, sub_keys.astype(jnp.float32))
        sv, si = lax.top_k(s, PEER_TOPK)
        cand = (sv[:, :, 0, :, None] + sv[:, :, 1, None, :]).reshape(PEER_BLOCK, PEER_HEADS, PEER_TOPK * PEER_TOPK)
        cidx = (si[:, :, 0, :, None] * PEER_NKEYS + si[:, :, 1, None, :]).reshape(PEER_BLOCK, PEER_HEADS, PEER_TOPK * PEER_TOPK)
        fv, fpos = lax.top_k(cand, PEER_TOPK)
        eidx = jnp.take_along_axis(cidx, fpos, axis=-1)
        gate = jax.nn.softmax(fv, axis=-1)
        act = jax.nn.gelu(jnp.einsum('td,thkd->thk', xb, exp_u[eidx]).astype(jnp.float32), approximate=False)
        return jnp.einsum('thk,thkd->td', (gate * act).astype(xb.dtype), exp_v[eidx])

    return lax.map(block, hb).reshape(b_, t_len, d)


def setup_inputs(seed: int = 0) -> dict:
    key = jax.random.key(seed)
    ks = jax.random.split(key, 24)
    f32 = jnp.float32

    def nrm(k, shape, s):
        return jax.random.normal(k, shape, f32) * s

    x = nrm(ks[0], (BATCH, SEQ, D_MODEL), 1.0)
    c = nrm(ks[1], (BATCH, D_MODEL), 1.0)
    ctx = nrm(ks[2], (BATCH, CTX_LEN, D_MODEL), 1.0)
    c_ctx = nrm(ks[3], (D_MODEL,), 1.0)
    w_mod = nrm(ks[4], (DEPTH, D_MODEL, 6 * D_MODEL), 0.5 * D_MODEL ** -0.5)
    b_mod = nrm(ks[5], (DEPTH, 6 * D_MODEL), 0.02)
    g_mix = 1.0 + nrm(ks[6], (DEPTH, D_MODEL), 0.02)
    w_in = nrm(ks[7], (DEPTH, D_MODEL, PROJ_WIDTH), D_MODEL ** -0.5)
    kb = jax.random.split(ks[8], 4)
    ml_i_b = nrm(kb[0], (DEPTH, N_DIR * ML_HEADS), 0.1)
    ml_f_b = jnp.tile(jnp.linspace(3.0, 6.0, ML_HEADS, dtype=f32), (DEPTH, N_DIR)) + nrm(kb[1], (DEPTH, N_DIR * ML_HEADS), 0.1)
    gd_b_b = nrm(kb[2], (DEPTH, N_DIR * GD_HEADS), 0.1)
    dt = jnp.exp(jax.random.uniform(kb[3], (DEPTH, N_DIR * GD_HEADS), f32, math.log(1e-3), math.log(1e-1)))
    gd_a_b = dt + jnp.log(-jnp.expm1(-dt))
    b_gate = jnp.concatenate([ml_i_b, ml_f_b, gd_b_b, gd_a_b], axis=-1)
    gd_conv_w = nrm(ks[9], (DEPTH, CONV_K, 3 * GD_WIDTH), CONV_K ** -0.5)
    gd_conv_b = nrm(ks[10], (DEPTH, 3 * GD_WIDTH), 0.02)
    gd_a_log = jnp.log(jax.random.uniform(ks[11], (DEPTH, N_DIR, GD_HEADS), f32, 1.0, 16.0))
    ml_norm_g = 1.0 + nrm(ks[12], (DEPTH, ML_HEADS, ML_DH), 0.02)
    gd_norm_g = 1.0 + nrm(ks[13], (DEPTH, GD_HEADS, GD_DH), 0.02)
    w_out = nrm(ks[14], (DEPTH, MIX_WIDTH, D_MODEL), MIX_WIDTH ** -0.5)
    g_ffn = 1.0 + nrm(ks[15], (DEPTH, D_MODEL), 0.02)
    peer_wq = nrm(ks[16], (DEPTH, D_MODEL, PEER_HEADS * PEER_DKEY), D_MODEL ** -0.5)
    peer_keys = nrm(ks[17], (DEPTH, PEER_HEADS, 2, PEER_NKEYS, PEER_HALF), PEER_HALF ** -0.5)
    peer_u = nrm(ks[18], (DEPTH, N_EXPERTS, D_MODEL), D_MODEL ** -0.5)
    peer_v = nrm(ks[19], (DEPTH, N_EXPERTS, D_MODEL), 1.0)
    g_final = 1.0 + nrm(ks[20], (D_MODEL,), 0.02)
    return {'x': x, 'c': c, 'ctx': ctx, 'c_ctx': c_ctx, 'w_mod': w_mod, 'b_mod': b_mod,
            'g_mix': g_mix, 'w_in': w_in, 'b_gate': b_gate, 'gd_conv_w': gd_conv_w,
            'gd_conv_b': gd_conv_b, 'gd_a_log': gd_a_log, 'ml_norm_g': ml_norm_g,
            'gd_norm_g': gd_norm_g, 'w_out': w_out, 'g_ffn': g_ffn, 'peer_wq': peer_wq,
            'peer_keys': peer_keys, 'peer_u': peer_u, 'peer_v': peer_v, 'g_final': g_final}


def reference(x, c, ctx, c_ctx, w_mod, b_mod, g_mix, w_in, b_gate, gd_conv_w, gd_conv_b,
              gd_a_log, ml_norm_g, gd_norm_g, w_out, g_ffn, peer_wq, peer_keys, peer_u,
              peer_v, g_final):
    s_lat = jax.nn.silu(c)[:, None, :]
    s_ctx = jax.nn.silu(c_ctx)
    for layer in range(DEPTH):
        last = layer == DEPTH - 1
        mod_lat = jnp.split(s_lat @ w_mod[layer] + b_mod[layer], 6, axis=-1)
        mod_ctx = jnp.split(s_ctx @ w_mod[layer] + b_mod[layer], 6, axis=-1)
        h_lat = _modulate(_rmsnorm(x, g_mix[layer]), mod_lat[0], mod_lat[1])
        h_ctx = _modulate(_rmsnorm(ctx, g_mix[layer]), mod_ctx[0], mod_ctx[1])
        y_lat, y_ctx = _mixer(h_lat, h_ctx, not last, w_in[layer], b_gate[layer], gd_conv_w[layer],
                              gd_conv_b[layer], gd_a_log[layer], ml_norm_g[layer], gd_norm_g[layer],
                              w_out[layer])
        x = x + mod_lat[2] * y_lat
        f_lat = _modulate(_rmsnorm(x, g_ffn[layer]), mod_lat[3], mod_lat[4])
        x = x + mod_lat[5] * _peer(f_lat, peer_wq[layer], peer_keys[layer], peer_u[layer], peer_v[layer])
        if not last:
            ctx = ctx + mod_ctx[2] * y_ctx
            f_ctx = _modulate(_rmsnorm(ctx, g_ffn[layer]), mod_ctx[3], mod_ctx[4])
            ctx = ctx + mod_ctx[5] * _peer(f_ctx, peer_wq[layer], peer_keys[layer], peer_u[layer], peer_v[layer])
    return _rmsnorm(x, g_final)
```

```python
import functools
import math

import jax
import jax.numpy as jnp
from jax import lax
from jax.experimental import pallas as pl
from jax.experimental.pallas import tpu as pltpu

f32 = jnp.float32
bf16 = jnp.bfloat16
i32 = jnp.int32

GRID_W = 64
PEER_TOPK = 16
EPS = 1e-6
NEG = -1e30
CHUNK = 128
LANES = 128
VMEM_CAP = 56 * 1024 * 1024


def _params(block_bytes, scratch_bytes=0, semantics=None):
    est = 2 * block_bytes + scratch_bytes + 16 * 1024 * 1024
    return pltpu.CompilerParams(
        dimension_semantics=semantics,
        vmem_limit_bytes=int(min(max(est, 32 * 1024 * 1024), VMEM_CAP)))


def _bdot(a, b):
    return jnp.dot(a.astype(bf16), b.astype(bf16), preferred_element_type=f32)


def _bdot_nt(a, b):
    return lax.dot_general(a.astype(bf16), b.astype(bf16), (((1,), (1,)), ((), ())),
                           preferred_element_type=f32)


def _bdot_tn(a, b):
    return lax.dot_general(a.astype(bf16), b.astype(bf16), (((0,), (0,)), ((), ())),
                           preferred_element_type=f32)


def _split3(x):
    hi = x.astype(bf16)
    r = x - hi.astype(f32)
    mid = r.astype(bf16)
    lo = (r - mid.astype(f32)).astype(bf16)
    return hi, mid, lo


def _softplus(x):
    return jnp.maximum(x, 0.0) + jnp.log1p(jnp.exp(-jnp.abs(x)))


def _sigmoid(x):
    return 1.0 / (1.0 + jnp.exp(-x))


def _silu(x):
    return x * _sigmoid(x)


def _mod_body(s_ref, w_ref, b_ref, o_ref):
    o_ref[...] = _bdot(_silu(s_ref[...]), w_ref[...]) + b_ref[...]


def _mod_vectors(s8, w_mod, b_mod):
    d, n = w_mod.shape
    tn = min(512, n)
    return pl.pallas_call(
        _mod_body,
        out_shape=jax.ShapeDtypeStruct((8, n), f32),
        grid=(n // tn,),
        in_specs=[pl.BlockSpec((8, d), lambda j: (0, 0)),
                  pl.BlockSpec((d, tn), lambda j: (0, j)),
                  pl.BlockSpec((1, tn), lambda j: (0, j))],
        out_specs=pl.BlockSpec((8, tn), lambda j: (0, j)),
        compiler_params=_params(d * tn * 4 + 8 * d * 4, semantics=("arbitrary",)),
        name="mod_vectors",
    )(s8, w_mod, b_mod)


def _norm_rows(x, g, shift, scale):
    y = x * lax.rsqrt(jnp.mean(x * x, axis=-1, keepdims=True) + EPS)
    return (y * g) * (1.0 + scale) + shift


def _normmod_body(x_ref, c_ref, g_ref, sh_ref, sc_ref, o_ref, *, n_lat):
    is_ctx = pl.program_id(0) >= n_lat
    x = jnp.where(is_ctx, c_ref[...], x_ref[...])
    sh = jnp.where(is_ctx, sh_ref[1:2, :], sh_ref[0:1, :])
    sc = jnp.where(is_ctx, sc_ref[1:2, :], sc_ref[0:1, :])
    o_ref[...] = _norm_rows(x, g_ref[...], sh, sc).astype(bf16)


def _normmod(x, ctx, g, mod, shift_blk, scale_blk, tm):
    t, d = x.shape
    n_lat = t // tm
    if ctx is None:
        ctx = x
        n_ctx = 0
    else:
        n_ctx = ctx.shape[0] // tm
    rows = (n_lat + n_ctx) * tm
    return pl.pallas_call(
        functools.partial(_normmod_body, n_lat=n_lat),
        out_shape=jax.ShapeDtypeStruct((rows, d), bf16),
        grid=(n_lat + n_ctx,),
        in_specs=[pl.BlockSpec((tm, d), lambda i: (jnp.minimum(i, n_lat - 1), 0)),
                  pl.BlockSpec((tm, d), lambda i: (jnp.maximum(i - n_lat, 0), 0)),
                  pl.BlockSpec((1, d), lambda i: (0, 0)),
                  pl.BlockSpec((8, d), lambda i: (0, shift_blk)),
                  pl.BlockSpec((8, d), lambda i: (0, scale_blk))],
        out_specs=pl.BlockSpec((tm, d), lambda i: (i, 0)),
        compiler_params=_params(3 * tm * d * 4, semantics=("arbitrary",)),
        name="normmod",
    )(x, ctx, g, mod, mod)


def _normmod_cm_body(x_ref, g_ref, sh_ref, sc_ref, o_ref, *, ncol):
    g = g_ref[...]
    sh = sh_ref[0:1, :]
    sc = sc_ref[0:1, :]
    for ci in range(ncol):
        o_ref[ci] = _norm_rows(x_ref[:, ci, :], g, sh, sc).astype(bf16)


def _normmod_colmajor(x3, g, mod, shift_blk, scale_blk, rb):
    r, w, d = x3.shape
    ncol = 8
    return pl.pallas_call(
        functools.partial(_normmod_cm_body, ncol=ncol),
        out_shape=jax.ShapeDtypeStruct((w, r, d), bf16),
        grid=(w // ncol, r // rb),
        in_specs=[pl.BlockSpec((rb, ncol, d), lambda c, j: (j, c, 0)),
                  pl.BlockSpec((1, d), lambda c, j: (0, 0)),
                  pl.BlockSpec((8, d), lambda c, j: (0, shift_blk)),
                  pl.BlockSpec((8, d), lambda c, j: (0, scale_blk))],
        out_specs=pl.BlockSpec((ncol, rb, d), lambda c, j: (c, j, 0)),
        compiler_params=_params(2 * rb * ncol * d * 4, semantics=("arbitrary", "arbitrary")),
        name="normmod_colmajor",
    )(x3, g, mod, mod)


def _mm_body(a_ref, b_ref, *rest, nt, epilogue):
    acc = (_bdot_nt if nt else _bdot)(a_ref[...], b_ref[...])
    o_ref = rest[-1]
    o_ref[...] = epilogue(acc, *rest[:-1]).astype(o_ref.dtype)


def _matmul(a, b, *, m, tm, tn, nt=False, a_off=0, extras=(), epilogue=None,
            out_dtype=f32, name="matmul"):
    k = a.shape[1]
    n = b.shape[0] if nt else b.shape[1]
    tn = min(tn, n)
    if epilogue is None:
        epilogue = lambda acc: acc
    b_spec = (pl.BlockSpec((tn, k), lambda i, j: (j, 0)) if nt
              else pl.BlockSpec((k, tn), lambda i, j: (0, j)))
    extra_bytes = sum(jnp.dtype(e.dtype).itemsize * math.prod(s.block_shape) for e, s in extras)
    return pl.pallas_call(
        functools.partial(_mm_body, nt=nt, epilogue=epilogue),
        out_shape=jax.ShapeDtypeStruct((m, n), out_dtype),
        grid=(m // tm, n // tn),
        in_specs=[pl.BlockSpec((tm, k), lambda i, j: (i + a_off, 0)), b_spec] + [s for _, s in extras],
        out_specs=pl.BlockSpec((tm, tn), lambda i, j: (i, j)),
        compiler_params=_params(tm * k * 2 + k * tn * 2 + tm * tn * 4 + extra_bytes,
                                semantics=("arbitrary", "arbitrary")),
        name=name,
    )(a, b, *[e for e, _ in extras])


def _gate_body(ga_ref, gb_ref, alog_ref, col_ref, row_ref, *, n_a, mh, gh):
    o_f, o_b, o_a = 2 * mh, 4 * mh, 4 * mh + 2 * gh
    is_b = pl.program_id(0) >= n_a
    g = jnp.where(is_b, gb_ref[...], ga_ref[...])
    n = g.shape[0]
    lane = lax.broadcasted_iota(i32, (n, LANES), 1)
    lf = -_softplus(-g)
    gl = -jnp.exp(alog_ref[...]) * _softplus(g)
    src = jnp.where(lane < o_b, lf, gl)
    rid = lax.broadcasted_iota(i32, (n, n), 0)
    cid = lax.broadcasted_iota(i32, (n, n), 1)
    tri_f = jnp.where(cid <= rid, 1.0, 0.0).astype(bf16)
    tri_b = jnp.where(cid >= rid, 1.0, 0.0).astype(bf16)
    parts = _split3(src)
    cf = sum(jnp.dot(tri_f, p, preferred_element_type=f32) for p in parts)
    cb = sum(jnp.dot(tri_b, p, preferred_element_type=f32) for p in parts)
    fwd = ((lane >= o_f) & (lane < o_f + mh)) | ((lane >= o_a) & (lane < o_a + gh))
    bwd = ((lane >= o_f + mh) & (lane < o_b)) | ((lane >= o_a + gh) & (lane < o_a + 2 * gh))
    beta = (lane >= o_b) & (lane < o_a)
    out = jnp.where(lane < o_f, g,
                    jnp.where(fwd, cf, jnp.where(bwd, cb, jnp.where(beta, _sigmoid(g), 0.0))))
    col_ref[...] = out
    row_ref[0] = out.T


def _gate_prep(g_a, n_a, g_b, b_off, n_b, alog_row, mh, gh):
    nc = n_a + n_b
    return pl.pallas_call(
        functools.partial(_gate_body, n_a=n_a, mh=mh, gh=gh),
        out_shape=(jax.ShapeDtypeStruct((nc * CHUNK, LANES), f32),
                   jax.ShapeDtypeStruct((nc, LANES, CHUNK), f32)),
        grid=(nc,),
        in_specs=[pl.BlockSpec((CHUNK, LANES), lambda i: (jnp.minimum(i, n_a - 1), 0)),
                  pl.BlockSpec((CHUNK, LANES), lambda i: (jnp.maximum(i - n_a, 0) + b_off, 0)),
                  pl.BlockSpec((1, LANES), lambda i: (0, 0))],
        out_specs=(pl.BlockSpec((CHUNK, LANES), lambda i: (i, 0)),
                   pl.BlockSpec((1, LANES, CHUNK), lambda i: (i, 0, 0))),
        compiler_params=_params(4 * CHUNK * LANES * 4, semantics=("arbitrary",)),
        name="gate_prep",
    )(g_a, g_b, alog_row)


def _fwd_chunk(j, n_lat, n_ctx):
    return jnp.where(j < n_ctx, n_lat + j, j - n_ctx)


def _bwd_chunk(j, n_lat, n_ctx):
    return n_lat + n_ctx - 1 - j


def _tri_masks(n):
    rid = lax.broadcasted_iota(i32, (n, n), 0)
    cid = lax.broadcasted_iota(i32, (n, n), 1)
    return cid <= rid, cid >= rid, cid < rid, cid > rid


def _mlstm_body(qf_ref, kf_ref, vf_ref, cf_ref, rf_ref, qb_ref, kb_ref, vb_ref, cb_ref, rb_ref,
                hf_ref, hb_ref, c_scr, n_scr, m_scr, *, mh, dh):
    @pl.when(pl.program_id(0) == 0)
    def _():
        c_scr[...] = jnp.zeros_like(c_scr)
        n_scr[...] = jnp.zeros_like(n_scr)
        m_scr[...] = jnp.full_like(m_scr, NEG)

    n = CHUNK
    incl_f, incl_b, _, _ = _tri_masks(n)
    scale = dh ** -0.5
    dirs = ((0, qf_ref, kf_ref, vf_ref, cf_ref, rf_ref, hf_ref, incl_f),
            (1, qb_ref, kb_ref, vb_ref, cb_ref, rb_ref, hb_ref, incl_b))
    for d, q_ref, k_ref, v_ref, col_ref, row_ref, h_ref, incl in dirs:
        for h in range(mh):
            ch = d * mh + h
            li, lb = d * mh + h, 2 * mh + d * mh + h
            sl = slice(h * dh, (h + 1) * dh)
            q = q_ref[:, sl] * scale
            k = k_ref[:, sl]
            v = v_ref[:, sl]
            icol = col_ref[:, li:li + 1]
            bcol = col_ref[:, lb:lb + 1]
            irow = row_ref[0, li:li + 1, :]
            brow = row_ref[0, lb:lb + 1, :]
            btot = brow[:, n - 1:n] if d == 0 else brow[:, 0:1]
            m_st = m_scr[ch, :, 0:1]
            c_st = c_scr[ch]
            n_st = n_scr[ch]
            dmat = jnp.where(incl, bcol - brow + irow, NEG)
            m_inter = bcol + m_st
            m_row = jnp.maximum(m_inter, jnp.max(dmat, axis=1, keepdims=True))
            s = _bdot_nt(q, k) * jnp.exp(dmat - m_row)
            w_inter = jnp.exp(m_inter - m_row)
            num = w_inter * _bdot(q, c_st) + _bdot(s, v)
            den = w_inter * jnp.sum(q * n_st, axis=1, keepdims=True) + jnp.sum(s, axis=1, keepdims=True)
            h_ref[:, sl] = num / jnp.maximum(jnp.abs(den), jnp.exp(-m_row))
            wlog = btot - bcol + icol
            m_new = jnp.maximum(btot + m_st, jnp.max(wlog, axis=0, keepdims=True))
            a_st = jnp.exp(btot + m_st - m_new)
            kw = k * jnp.exp(wlog - m_new)
            c_scr[ch] = a_st * c_st + _bdot_tn(kw, v)
            n_scr[ch] = a_st * n_st + jnp.sum(kw, axis=0, keepdims=True)
            m_scr[ch] = jnp.broadcast_to(m_new, (1, LANES))


def _mlstm_scan(pm, gcol, grow, n_lat, n_ctx, mh, dh):
    rows = pm.shape[0]
    mlw = mh * dh
    nc = n_lat + n_ctx
    fc = functools.partial(_fwd_chunk, n_lat=n_lat, n_ctx=n_ctx)
    bc = functools.partial(_bwd_chunk, n_lat=n_lat, n_ctx=n_ctx)

    def specs(cm):
        return [pl.BlockSpec((CHUNK, mlw), lambda j, p=p: (cm(j), p)) for p in range(3)] + [
            pl.BlockSpec((CHUNK, LANES), lambda j: (cm(j), 0)),
            pl.BlockSpec((1, LANES, CHUNK), lambda j: (cm(j), 0, 0))]

    blk = 2 * (3 * CHUNK * mlw * 4 + 2 * CHUNK * LANES * 4 + CHUNK * mlw * 4)
    scr = 2 * mh * (dh * dh + 8 * dh + 8 * LANES) * 4
    return pl.pallas_call(
        functools.partial(_mlstm_body, mh=mh, dh=dh),
        out_shape=(jax.ShapeDtypeStruct((rows, mlw), f32), jax.ShapeDtypeStruct((rows, mlw), f32)),
        grid=(nc,),
        in_specs=specs(fc) + specs(bc),
        out_specs=(pl.BlockSpec((CHUNK, mlw), lambda j: (fc(j), 0)),
                   pl.BlockSpec((CHUNK, mlw), lambda j: (bc(j), 0))),
        scratch_shapes=[pltpu.VMEM((2 * mh, dh, dh), f32),
                        pltpu.VMEM((2 * mh, 1, dh), f32),
                        pltpu.VMEM((2 * mh, 1, LANES), f32)],
        compiler_params=_params(blk, scr, semantics=("arbitrary",)),
        name="mlstm_scan",
    )(pm, pm, pm, gcol, grow, pm, pm, pm, gcol, grow)


def _gdn_conv_body(l_ref, c_ref, p_ref, n_ref, w_ref, b_ref, o_ref, ext, *, n_lat, kw, gdw, dk):
    b = pl.program_id(0)
    cw = o_ref.shape[1]
    rb = o_ref.shape[0]
    is_ctx = b >= n_lat
    half = kw // 2
    ext[0:8, :] = jnp.where((b == 0) | is_ctx, 0.0, p_ref[...])
    ext[8:8 + rb, :] = jnp.where(is_ctx, c_ref[...], l_ref[...])
    ext[8 + rb:16 + rb, :] = jnp.where(b >= n_lat - 1, 0.0, n_ref[...])
    y = jnp.broadcast_to(b_ref[...], (rb, cw))
    for j in range(kw):
        y = y + ext[pl.ds(8 - half + j, rb), :] * w_ref[j:j + 1, :]
    y = _silu(y)
    part = (pl.program_id(1) * cw) // gdw
    qk_scale = jnp.where(part == 0, dk ** -0.5, 1.0)
    for hh in range(cw // dk):
        sl = slice(hh * dk, (hh + 1) * dk)
        t = y[:, sl]
        tn = t * lax.rsqrt(jnp.sum(t * t, axis=-1, keepdims=True) + EPS) * qk_scale
        o_ref[:, sl] = jnp.where(part == 2, t, tn)


def _gdn_conv(pg_lat, pg_ctx, conv_w, conv_b, gdw, dk):
    t = pg_lat.shape[0]
    rb = pg_ctx.shape[0]
    n_lat = t // rb
    kw = conv_w.shape[0]
    cw = min(512, gdw)
    nh8 = rb // 8
    return pl.pallas_call(
        functools.partial(_gdn_conv_body, n_lat=n_lat, kw=kw, gdw=gdw, dk=dk),
        out_shape=jax.ShapeDtypeStruct((t + rb, 3 * gdw), f32),
        grid=(n_lat + 1, 3 * gdw // cw),
        in_specs=[pl.BlockSpec((rb, cw), lambda b, c: (jnp.minimum(b, n_lat - 1), c)),
                  pl.BlockSpec((rb, cw), lambda b, c: (0, c)),
                  pl.BlockSpec((8, cw), lambda b, c: (jnp.clip(b * nh8 - 1, 0, n_lat * nh8 - 1), c)),
                  pl.BlockSpec((8, cw), lambda b, c: (jnp.minimum((b + 1) * nh8, n_lat * nh8 - 1), c)),
                  pl.BlockSpec((kw, cw), lambda b, c: (0, c)),
                  pl.BlockSpec((1, cw), lambda b, c: (0, c))],
        out_specs=pl.BlockSpec((rb, cw), lambda b, c: (b, c)),
        scratch_shapes=[pltpu.VMEM((rb + 16, cw), f32)],
        compiler_params=_params(4 * rb * cw * 4, (rb + 16) * cw * 4, semantics=("arbitrary", "arbitrary")),
        name="gdn_conv",
    )(pg_lat, pg_ctx, pg_lat, pg_lat, conv_w, conv_b)


def _unit_tri_inverse(a, n):
    rid = lax.broadcasted_iota(i32, (n, n), 0)
    cid = lax.broadcasted_iota(i32, (n, n), 1)
    t = jnp.where(rid == cid, 1.0, 0.0) - jnp.where((rid >> 1) == (cid >> 1), a, 0.0)
    sb = 1
    while (1 << sb) < n:
        off = ((rid >> (sb + 1)) == (cid >> (sb + 1))) & ((rid >> sb) != (cid >> sb))
        t = t - _bdot(t, _bdot(jnp.where(off, a, 0.0), t))
        sb += 1
    return t


def _gdn_body(qf_ref, kf_ref, vf_ref, cf_ref, rf_ref, qb_ref, kb_ref, vb_ref, cb_ref, rb_ref,
              of_ref, ob_ref, s_scr, *, mh, gh, dk):
    @pl.when(pl.program_id(0) == 0)
    def _():
        s_scr[...] = jnp.zeros_like(s_scr)

    n = CHUNK
    incl_f, incl_b, strict_f, strict_b = _tri_masks(n)
    o_b, o_a = 4 * mh, 4 * mh + 2 * gh
    dirs = ((0, qf_ref, kf_ref, vf_ref, cf_ref, rf_ref, of_ref, incl_f, strict_f),
            (1, qb_ref, kb_ref, vb_ref, cb_ref, rb_ref, ob_ref, incl_b, strict_b))
    for d, q_ref, k_ref, v_ref, col_ref, row_ref, o_ref, incl, strict in dirs:
        for h in range(gh):
            ch = d * gh + h
            lbeta, lg = o_b + d * gh + h, o_a + d * gh + h
            sl = slice(h * dk, (h + 1) * dk)
            q = q_ref[:, sl]
            k = k_ref[:, sl]
            v = v_ref[:, sl]
            beta = col_ref[:, lbeta:lbeta + 1]
            gcol = col_ref[:, lg:lg + 1]
            grow = row_ref[0, lg:lg + 1, :]
            glast = grow[:, n - 1:n] if d == 0 else grow[:, 0:1]
            decay = jnp.exp(jnp.where(incl, gcol - grow, NEG))
            kb = k * beta
            egc = jnp.exp(gcol)
            tinv = _unit_tri_inverse(jnp.where(strict, _bdot_nt(kb, k) * decay, 0.0), n)
            w = _bdot(tinv, kb * egc)
            u = _bdot(tinv, v * beta)
            attn = _bdot_nt(q, k) * decay
            s_st = s_scr[ch]
            v_new = u - _bdot(w, s_st)
            o_ref[:, sl] = _bdot(q * egc, s_st) + _bdot(attn, v_new)
            kd = k * jnp.exp(glast - gcol)
            s_scr[ch] = jnp.exp(glast) * s_st + _bdot_tn(kd, v_new)


def _gdn_scan(qkv, gcol, grow, n_lat, n_ctx, mh, gh, dk):
    rows = qkv.shape[0]
    gdw = gh * dk
    nc = n_lat + n_ctx
    fc = functools.partial(_fwd_chunk, n_lat=n_lat, n_ctx=n_ctx)
    bc = functools.partial(_bwd_chunk, n_lat=n_lat, n_ctx=n_ctx)

    def specs(cm):
        return [pl.BlockSpec((CHUNK, gdw), lambda j, p=p: (cm(j), p)) for p in range(3)] + [
            pl.BlockSpec((CHUNK, LANES), lambda j: (cm(j), 0)),
            pl.BlockSpec((1, LANES, CHUNK), lambda j: (cm(j), 0, 0))]

    blk = 2 * (3 * CHUNK * gdw * 4 + 2 * CHUNK * LANES * 4 + CHUNK * gdw * 4)
    return pl.pallas_call(
        functools.partial(_gdn_body, mh=mh, gh=gh, dk=dk),
        out_shape=(jax.ShapeDtypeStruct((rows, gdw), f32), jax.ShapeDtypeStruct((rows, gdw), f32)),
        grid=(nc,),
        in_specs=specs(fc) + specs(bc),
        out_specs=(pl.BlockSpec((CHUNK, gdw), lambda j: (fc(j), 0)),
                   pl.BlockSpec((CHUNK, gdw), lambda j: (bc(j), 0))),
        scratch_shapes=[pltpu.VMEM((2 * gh, dk, dk), f32)],
        compiler_params=_params(blk, 2 * gh * dk * dk * 4, semantics=("arbitrary",)),
        name="gdn_scan",
    )(qkv, qkv, qkv, gcol, grow, qkv, qkv, qkv, gcol, grow)


def _head_rms(t, g):
    return t * lax.rsqrt(jnp.mean(t * t, axis=-1, keepdims=True) + EPS) * g


def _mix_ml_body(hf_ref, hb_ref, o_gate_ref, g_ref, o_ref, *, mh, dh):
    for h in range(mh):
        sl = slice(h * dh, (h + 1) * dh)
        y = _head_rms(hf_ref[:, sl] + hb_ref[:, sl], g_ref[:, sl])
        o_ref[:, sl] = (_sigmoid(o_gate_ref[:, sl]) * y).astype(bf16)


def _mix_ml(hf, hb, pm, g_row, t, mh, dh, tm):
    mlw = mh * dh
    return pl.pallas_call(
        functools.partial(_mix_ml_body, mh=mh, dh=dh),
        out_shape=jax.ShapeDtypeStruct((t, mlw), bf16),
        grid=(t // tm,),
        in_specs=[pl.BlockSpec((tm, mlw), lambda i: (i, 0)),
                  pl.BlockSpec((tm, mlw), lambda i: (i, 0)),
                  pl.BlockSpec((tm, mlw), lambda i: (i, 3)),
                  pl.BlockSpec((1, mlw), lambda i: (0, 0))],
        out_specs=pl.BlockSpec((tm, mlw), lambda i: (i, 0)),
        compiler_params=_params(4 * tm * mlw * 4, semantics=("arbitrary",)),
        name="mix_ml",
    )(hf, hb, pm, g_row)


def _mix_gd_body(hf_ref, hb_ref, z_ref, g_ref, o_ref, *, gh, dk, ncol):
    for ci in range(ncol):
        for h in range(gh):
            sl = slice(h * dk, (h + 1) * dk)
            y = _head_rms(hf_ref[ci, :, sl] + hb_ref[ci, :, sl], g_ref[:, sl])
            o_ref[:, ci, sl] = (_silu(z_ref[ci, :, sl]) * y).astype(bf16)


def _mix_gd(hf3, hb3, pg3, g_row, r, w, gh, dk, rb):
    gdw = gh * dk
    ncol = 16
    return pl.pallas_call(
        functools.partial(_mix_gd_body, gh=gh, dk=dk, ncol=ncol),
        out_shape=jax.ShapeDtypeStruct((r, w, gdw), bf16),
        grid=(w // ncol, r // rb),
        in_specs=[pl.BlockSpec((ncol, rb, gdw), lambda c, j: (c, j, 0)),
                  pl.BlockSpec((ncol, rb, gdw), lambda c, j: (c, j, 0)),
                  pl.BlockSpec((ncol, rb, gdw), lambda c, j: (c, j, 3)),
                  pl.BlockSpec((1, gdw), lambda c, j: (0, 0))],
        out_specs=pl.BlockSpec((rb, ncol, gdw), lambda c, j: (j, c, 0)),
        compiler_params=_params(4 * ncol * rb * gdw * 4, semantics=("arbitrary", "arbitrary")),
        name="mix_gd",
    )(hf3, hb3, pg3, g_row)


def _outproj_body(a1_ref, a2_ref, b1_ref, b2_ref, x_ref, gate_ref, o_ref):
    y = _bdot(a1_ref[...], b1_ref[...]) + _bdot(a2_ref[...], b2_ref[...])
    o_ref[...] = x_ref[...] + gate_ref[0:1, :] * y


def _outproj(a1, a2, w1, w2, x, mod, gate_blk, tm, tn):
    t, k1 = a1.shape
    k2 = a2.shape[1]
    d = x.shape[1]
    tn = min(tn, d)
    gb = gate_blk * (d // tn)
    return pl.pallas_call(
        _outproj_body,
        out_shape=jax.ShapeDtypeStruct((t, d), f32),
        grid=(t // tm, d // tn),
        in_specs=[pl.BlockSpec((tm, k1), lambda i, j: (i, 0)),
                  pl.BlockSpec((tm, k2), lambda i, j: (i, 0)),
                  pl.BlockSpec((k1, tn), lambda i, j: (0, j)),
                  pl.BlockSpec((k2, tn), lambda i, j: (0, j)),
                  pl.BlockSpec((tm, tn), lambda i, j: (i, j)),
                  pl.BlockSpec((8, tn), lambda i, j: (0, gb + j))],
        out_specs=pl.BlockSpec((tm, tn), lambda i, j: (i, j)),
        compiler_params=_params(tm * (k1 + k2) * 2 + (k1 + k2) * tn * 2 + 2 * tm * tn * 4,
                                semantics=("arbitrary", "arbitrary")),
        name="outproj",
    )(a1, a2, w1, w2, x, mod)


def _topk_rows(s, kk, n_rows):
    rid = lax.broadcasted_iota(i32, s.shape, 0)
    vals, idxs = [], []
    for _ in range(kk):
        m = jnp.max(s, axis=0, keepdims=True)
        idx = jnp.min(jnp.where(s == m, rid, n_rows), axis=0, keepdims=True)
        s = jnp.where(rid == idx, -jnp.inf, s)
        vals.append(m)
        idxs.append(idx)
    return vals, idxs


def _peer_topk_body(q_ref, keys_ref, e_ref, g_ref, *, ph, nk, half):
    kk = PEER_TOPK
    tb = q_ref.shape[0]
    e_rows, g_rows = [], []
    for h in range(ph):
        sv, si = [], []
        for p in range(2):
            grp = 2 * h + p
            qg = q_ref[:, grp * half:(grp + 1) * half]
            kg = keys_ref[grp]
            k_hi, k_mid, _ = _split3(kg)
            q_hi, q_mid, _ = _split3(qg)
            nt = lambda a, b: lax.dot_general(a, b, (((1,), (1,)), ((), ())), preferred_element_type=f32)
            s = nt(k_hi, q_hi) + (nt(k_hi, q_mid) + nt(k_mid, q_hi))
            vals, idxs = _topk_rows(s, kk, nk)
            sv.append(vals)
            si.append(idxs)
        sv1 = jnp.concatenate(sv[1], axis=0)
        si1 = jnp.concatenate(si[1], axis=0)
        cand = jnp.concatenate([sv[0][a] + sv1 for a in range(kk)], axis=0)
        eidx = jnp.concatenate([si[0][a] * nk + si1 for a in range(kk)], axis=0)
        rid = lax.broadcasted_iota(i32, cand.shape, 0)
        fv, fe = [], []
        for _ in range(kk):
            m = jnp.max(cand, axis=0, keepdims=True)
            pos = jnp.min(jnp.where(cand == m, rid, kk * kk), axis=0, keepdims=True)
            hit = rid == pos
            fe.append(jnp.max(jnp.where(hit, eidx, -1), axis=0, keepdims=True))
            cand = jnp.where(hit, -jnp.inf, cand)
            fv.append(m)
        fv = jnp.concatenate(fv, axis=0)
        ex = jnp.exp(fv - fv[0:1, :])
        g_rows.append(ex / jnp.sum(ex, axis=0, keepdims=True))
        e_rows.append(jnp.concatenate(fe, axis=0))
    e_ref[...] = jnp.concatenate(e_rows, axis=0).T
    g_ref[...] = jnp.concatenate(g_rows, axis=0).T


def _peer_topk(qp, keys, ph, nk, half, tb):
    t = qp.shape[0]
    hk = ph * PEER_TOPK
    return pl.pallas_call(
        functools.partial(_peer_topk_body, ph=ph, nk=nk, half=half),
        out_shape=(jax.ShapeDtypeStruct((t, hk), i32), jax.ShapeDtypeStruct((t, hk), f32)),
        grid=(t // tb,),
        in_specs=[pl.BlockSpec((tb, 2 * ph * half), lambda i: (i, 0)),
                  pl.BlockSpec((2 * ph, nk, half), lambda i: (0, 0, 0))],
        out_specs=(pl.BlockSpec((tb, hk), lambda i: (i, 0)), pl.BlockSpec((tb, hk), lambda i: (i, 0))),
        compiler_params=_params(tb * 2 * ph * half * 4 + 2 * ph * nk * half * 4 + 2 * tb * hk * 4,
                                semantics=("arbitrary",)),
        name="peer_topk",
    )(qp, keys)


def _peer_w_body(e_ref, g_ref, w_ref, *, nk):
    tb, hk = e_ref.shape
    shift = nk.bit_length() - 1
    sub = lax.broadcasted_iota(i32, (nk, hk), 0)

    def step(t, carry):
        e = e_ref[pl.ds(t, 1), :]
        g = g_ref[pl.ds(t, 1), :]
        ei = lax.shift_right_logical(e, shift)
        ej = e & (nk - 1)
        g_hi = g.astype(bf16).astype(f32)
        pt = jnp.where(sub == ei, 1.0, 0.0).astype(bf16)
        hit = sub == ej
        qt_hi = jnp.where(hit, g_hi, 0.0).astype(bf16)
        qt_lo = jnp.where(hit, g - g_hi, 0.0).astype(bf16)
        nt = lambda a, b: lax.dot_general(a, b, (((1,), (1,)), ((), ())), preferred_element_type=f32)
        w_ref[t] = nt(pt, qt_hi) + nt(pt, qt_lo)
        return carry

    lax.fori_loop(0, tb, step, 0)


def _peer_weights(e, g, nk, tb):
    t, hk = e.shape
    return pl.pallas_call(
        functools.partial(_peer_w_body, nk=nk),
        out_shape=jax.ShapeDtypeStruct((t, nk, nk), f32),
        grid=(t // tb,),
        in_specs=[pl.BlockSpec((tb, hk), lambda i: (i, 0)), pl.BlockSpec((tb, hk), lambda i: (i, 0))],
        out_specs=pl.BlockSpec((tb, nk, nk), lambda i: (i, 0, 0)),
        compiler_params=_params(tb * nk * nk * 4 + 2 * tb * hk * 4, semantics=("arbitrary",)),
        name="peer_weights",
    )(e, g)


def _gelu(x):
    return 0.5 * x * (1.0 + lax.erf(x * 0.7071067811865476))


def _peer_act_body(f_ref, u_ref, w_ref, o_ref, *, nk, gi):
    acc = _bdot_nt(f_ref[...], u_ref[...])
    for gg in range(gi):
        sl = slice(gg * nk, (gg + 1) * nk)
        o_ref[:, sl] = (w_ref[:, gg, :] * _gelu(acc[:, sl])).astype(bf16)


def _peer_act(f, u, w3, nk, tm):
    t, d = f.shape
    ne = u.shape[0]
    gi = 8
    tn = gi * nk
    return pl.pallas_call(
        functools.partial(_peer_act_body, nk=nk, gi=gi),
        out_shape=jax.ShapeDtypeStruct((t, ne), bf16),
        grid=(t // tm, ne // tn),
        in_specs=[pl.BlockSpec((tm, d), lambda i, j: (i, 0)),
                  pl.BlockSpec((tn, d), lambda i, j: (j, 0)),
                  pl.BlockSpec((tm, gi, nk), lambda i, j: (i, j, 0))],
        out_specs=pl.BlockSpec((tm, tn), lambda i, j: (i, j)),
        compiler_params=_params(tm * d * 2 + tn * d * 2 + tm * tn * 4 + tm * tn * 2 + tm * tn * 4,
                                semantics=("arbitrary", "arbitrary")),
        name="peer_act",
    )(f, u, w3)


def _peer_out_body(a_ref, v_ref, x_ref, gate_ref, o_ref, acc):
    kq = pl.program_id(2)

    @pl.when(kq == 0)
    def _():
        acc[...] = jnp.zeros_like(acc)

    acc[...] += jnp.dot(a_ref[...], v_ref[...], preferred_element_type=f32)

    @pl.when(kq == pl.num_programs(2) - 1)
    def _():
        o_ref[...] = x_ref[...] + gate_ref[0:1, :] * acc[...]


def _peer_out(a, v, x, mod, gate_blk, tm, tn, tk):
    t, ne = a.shape
    d = v.shape[1]
    tn = min(tn, d)
    tk = min(tk, ne)
    gb = gate_blk * (d // tn)
    return pl.pallas_call(
        _peer_out_body,
        out_shape=jax.ShapeDtypeStruct((t, d), f32),
        grid=(t // tm, d // tn, ne // tk),
        in_specs=[pl.BlockSpec((tm, tk), lambda i, j, k: (i, k)),
                  pl.BlockSpec((tk, tn), lambda i, j, k: (k, j)),
                  pl.BlockSpec((tm, tn), lambda i, j, k: (i, j)),
                  pl.BlockSpec((8, tn), lambda i, j, k: (0, gb + j))],
        out_specs=pl.BlockSpec((tm, tn), lambda i, j, k: (i, j)),
        scratch_shapes=[pltpu.VMEM((tm, tn), f32)],
        compiler_params=_params(tm * tk * 2 + tk * tn * 2 + 2 * tm * tn * 4, tm * tn * 4,
                                semantics=("arbitrary", "arbitrary", "arbitrary")),
        name="peer_out",
    )(a, v, x, mod)


def _final_norm_body(x_ref, g_ref, o_ref):
    x = x_ref[...]
    o_ref[...] = x * lax.rsqrt(jnp.mean(x * x, axis=-1, keepdims=True) + EPS) * g_ref[...]


def _final_norm(x, g, tm):
    t, d = x.shape
    return pl.pallas_call(
        _final_norm_body,
        out_shape=jax.ShapeDtypeStruct((t, d), f32),
        grid=(t // tm,),
        in_specs=[pl.BlockSpec((tm, d), lambda i: (i, 0)), pl.BlockSpec((1, d), lambda i: (0, 0))],
        out_specs=pl.BlockSpec((tm, d), lambda i: (i, 0)),
        compiler_params=_params(2 * tm * d * 4, semantics=("arbitrary",)),
        name="final_norm",
    )(x, g)


def _pick(n, pref):
    t = min(pref, n)
    while n % t:
        t -= 8
    return t


def kernel(x, c, ctx, c_ctx, w_mod, b_mod, g_mix, w_in, b_gate, gd_conv_w, gd_conv_b, gd_a_log,
           ml_norm_g, gd_norm_g, w_out, g_ffn, peer_wq, peer_keys, peer_u, peer_v, g_final):
    return _block(x, c, ctx, c_ctx, w_mod, b_mod, g_mix, w_in, b_gate, gd_conv_w, gd_conv_b, gd_a_log,
                  ml_norm_g, gd_norm_g, w_out, g_ffn, peer_wq, peer_keys, peer_u, peer_v, g_final)[0]


def _block(x, c, ctx, c_ctx, w_mod, b_mod, g_mix, w_in, b_gate, gd_conv_w, gd_conv_b, gd_a_log,
           ml_norm_g, gd_norm_g, w_out, g_ffn, peer_wq, peer_keys, peer_u, peer_v, g_final):
    assert w_mod.shape[0] == 1 and x.shape[0] == 1, "one layer, one batch element"
    _, t, d = x.shape
    tc = ctx.shape[1]
    _, mh, dh = ml_norm_g.shape
    _, gh, dk = gd_norm_g.shape
    mlw, gdw = mh * dh, gh * dk
    _, ph, _, nk, half = peer_keys.shape
    ne = peer_u.shape[1]
    w = GRID_W
    r = t // w
    assert r == tc and t % CHUNK == 0 and tc % CHUNK == 0 and r % CHUNK == 0
    assert ne == nk * nk and nk & (nk - 1) == 0 and 4 * mh + 4 * gh <= LANES
    n_lat, n_ctx = t // CHUNK, tc // CHUNK

    x2 = x.reshape(t, d)
    ctx2 = ctx.reshape(tc, d)

    s8 = jnp.zeros((8, d), f32).at[0].set(c[0]).at[1].set(c_ctx)
    mod = _mod_vectors(s8, w_mod[0], b_mod)

    h_rm = _normmod(x2, ctx2, g_mix, mod, 0, 1, tm=tc)
    h_cm = _normmod_colmajor(x2.reshape(r, w, d), g_mix[0:1], mod, 0, 1, rb=_pick(r, 64)).reshape(t, d)

    main = 4 * mlw + 4 * gdw
    w_in0 = w_in[0]
    w_ml = w_in0[:, :4 * mlw].astype(bf16)
    w_gd = w_in0[:, 4 * mlw:main].astype(bf16)
    ng = w_in0.shape[1] - main
    w_gate = jnp.pad(w_in0[:, main:], ((0, 0), (0, LANES - ng))).astype(bf16)
    b_gate_row = jnp.pad(b_gate, ((0, 0), (0, LANES - ng)))
    rows = t + tc
    tm_all = _pick(rows, 1280)
    tm_lat = _pick(t, 1024)
    pm = _matmul(h_rm, w_ml, m=rows, tm=tm_all, tn=512, name="proj_ml")
    pg_lat = _matmul(h_cm, w_gd, m=t, tm=tm_lat, tn=512, name="proj_gd_lat")
    pg_ctx = _matmul(h_rm, w_gd, m=tc, tm=tc, tn=512, a_off=t // tc, name="proj_gd_ctx")
    bias_spec = pl.BlockSpec((1, LANES), lambda i, j: (0, 0))
    add_bias = lambda acc, b_ref: acc + b_ref[...]
    g_rm = _matmul(h_rm, w_gate, m=rows, tm=tm_all, tn=LANES, extras=[(b_gate_row, bias_spec)],
                   epilogue=add_bias, name="proj_gate_rm")
    g_cm = _matmul(h_cm, w_gate, m=t, tm=tm_lat, tn=LANES, extras=[(b_gate_row, bias_spec)],
                   epilogue=add_bias, name="proj_gate_cm")

    o_a = 4 * mh + 2 * gh
    alog_row = jnp.zeros((1, LANES), f32).at[0, o_a:o_a + 2 * gh].set(gd_a_log.reshape(-1))
    ml_col, ml_row = _gate_prep(g_rm, n_lat, g_rm, n_lat, n_ctx, alog_row, mh, gh)
    gd_col, gd_row = _gate_prep(g_cm, n_lat, g_rm, n_lat, n_ctx, alog_row, mh, gh)

    hm_f, hm_b = _mlstm_scan(pm, ml_col, ml_row, n_lat, n_ctx, mh, dh)

    qkv = _gdn_conv(pg_lat, pg_ctx, gd_conv_w[0], gd_conv_b, gdw, dk)
    hg_f, hg_b = _gdn_scan(qkv, gd_col, gd_row, n_lat, n_ctx, mh, gh, dk)

    mix_ml = _mix_ml(hm_f, hm_b, pm, ml_norm_g.reshape(1, mlw), t, mh, dh, tm=_pick(t, 256))
    ncols = rows // r
    mix_gd = _mix_gd(hg_f.reshape(ncols, r, gdw), hg_b.reshape(ncols, r, gdw),
                     pg_lat.reshape(w, r, 4 * gdw), gd_norm_g.reshape(1, gdw), r, w, gh, dk,
                     rb=_pick(r, 32)).reshape(t, gdw)
    w_out0 = w_out[0].astype(bf16)
    x1 = _outproj(mix_ml, mix_gd, w_out0[:mlw], w_out0[mlw:], x2, mod, 2, tm=tm_lat, tn=512)

    f_in = _normmod(x1, None, g_ffn, mod, 3, 4, tm=_pick(t, 256))
    qp = _matmul(f_in, peer_wq[0].astype(bf16), m=t, tm=tm_lat, tn=512, name="peer_query")
    e_idx, gates = _peer_topk(qp, peer_keys[0].reshape(2 * ph, nk, half), ph, nk, half, tb=_pick(t, 256))
    w3 = _peer_weights(e_idx, gates, nk, tb=_pick(t, 128))
    act = _peer_act(f_in, peer_u[0].astype(bf16), w3, nk, tm=_pick(t, 512))
    x2_out = _peer_out(act, peer_v[0].astype(bf16), x1, mod, 5, tm=_pick(t, 512), tn=2048, tk=1024)

    out = _final_norm(x2_out, g_final.reshape(1, d), tm=_pick(t, 256))
    aux = dict(mod=mod, h_rm=h_rm, h_cm=h_cm, pm=pm, pg_lat=pg_lat, pg_ctx=pg_ctx, g_rm=g_rm, g_cm=g_cm,
               ml_col=ml_col, ml_row=ml_row, gd_col=gd_col, gd_row=gd_row, hm_f=hm_f, hm_b=hm_b, qkv=qkv,
               hg_f=hg_f, hg_b=hg_b, mix_ml=mix_ml, mix_gd=mix_gd, x1=x1, f_in=f_in, qp=qp, e_idx=e_idx,
               gates=gates, w3=w3, act=act, x2_out=x2_out)
    return out.reshape(1, t, d), aux
```

```python
import functools
import math

import jax
import jax.numpy as jnp
from jax import lax
from jax.experimental import pallas as pl
from jax.experimental.pallas import tpu as pltpu

f32 = jnp.float32
bf16 = jnp.bfloat16
i32 = jnp.int32

GRID_W = 64
PEER_TOPK = 16
EPS = 1e-6
NEG = -1e30
CHUNK = 128
LANES = 128
VMEM_CAP = 56 * 1024 * 1024


def _params(block_bytes, scratch_bytes=0, semantics=None):
    est = 2 * block_bytes + scratch_bytes + 16 * 1024 * 1024
    return pltpu.CompilerParams(
        dimension_semantics=semantics,
        vmem_limit_bytes=int(min(max(est, 32 * 1024 * 1024), VMEM_CAP)))


def _bdot(a, b):
    return jnp.dot(a.astype(bf16), b.astype(bf16), preferred_element_type=f32)


def _bdot_nt(a, b):
    return lax.dot_general(a.astype(bf16), b.astype(bf16), (((1,), (1,)), ((), ())),
                           preferred_element_type=f32)


def _bdot_tn(a, b):
    return lax.dot_general(a.astype(bf16), b.astype(bf16), (((0,), (0,)), ((), ())),
                           preferred_element_type=f32)


def _split3(x):
    hi = x.astype(bf16)
    r = x - hi.astype(f32)
    mid = r.astype(bf16)
    lo = (r - mid.astype(f32)).astype(bf16)
    return hi, mid, lo


def _softplus(x):
    return jnp.maximum(x, 0.0) + jnp.log1p(jnp.exp(-jnp.abs(x)))


def _sigmoid(x):
    return 1.0 / (1.0 + jnp.exp(-x))


def _silu(x):
    return x * _sigmoid(x)


def _mod_body(s_ref, w_ref, b_ref, o_ref):
    o_ref[...] = _bdot(_silu(s_ref[...]), w_ref[...]) + b_ref[...]


def _mod_vectors(s8, w_mod, b_mod):
    d, n = w_mod.shape
    tn = min(512, n)
    return pl.pallas_call(
        _mod_body,
        out_shape=jax.ShapeDtypeStruct((8, n), f32),
        grid=(n // tn,),
        in_specs=[pl.BlockSpec((8, d), lambda j: (0, 0)),
                  pl.BlockSpec((d, tn), lambda j: (0, j)),
                  pl.BlockSpec((1, tn), lambda j: (0, j))],
        out_specs=pl.BlockSpec((8, tn), lambda j: (0, j)),
        compiler_params=_params(d * tn * 4 + 8 * d * 4, semantics=("arbitrary",)),
        name="mod_vectors",
    )(s8, w_mod, b_mod)


def _norm_rows(x, g, shift, scale):
    y = x * lax.rsqrt(jnp.mean(x * x, axis=-1, keepdims=True) + EPS)
    return (y * g) * (1.0 + scale) + shift


def _normmod_body(x_ref, c_ref, g_ref, sh_ref, sc_ref, o_ref, *, n_lat):
    is_ctx = pl.program_id(0) >= n_lat
    x = jnp.where(is_ctx, c_ref[...], x_ref[...])
    sh = jnp.where(is_ctx, sh_ref[1:2, :], sh_ref[0:1, :])
    sc = jnp.where(is_ctx, sc_ref[1:2, :], sc_ref[0:1, :])
    o_ref[...] = _norm_rows(x, g_ref[...], sh, sc).astype(bf16)


def _normmod(x, ctx, g, mod, shift_blk, scale_blk, tm):
    t, d = x.shape
    n_lat = t // tm
    if ctx is None:
        ctx = x
        n_ctx = 0
    else:
        n_ctx = ctx.shape[0] // tm
    rows = (n_lat + n_ctx) * tm
    return pl.pallas_call(
        functools.partial(_normmod_body, n_lat=n_lat),
        out_shape=jax.ShapeDtypeStruct((rows, d), bf16),
        grid=(n_lat + n_ctx,),
        in_specs=[pl.BlockSpec((tm, d), lambda i: (jnp.minimum(i, n_lat - 1), 0)),
                  pl.BlockSpec((tm, d), lambda i: (jnp.maximum(i - n_lat, 0), 0)),
                  pl.BlockSpec((1, d), lambda i: (0, 0)),
                  pl.BlockSpec((8, d), lambda i: (0, shift_blk)),
                  pl.BlockSpec((8, d), lambda i: (0, scale_blk))],
        out_specs=pl.BlockSpec((tm, d), lambda i: (i, 0)),
        compiler_params=_params(3 * tm * d * 4, semantics=("arbitrary",)),
        name="normmod",
    )(x, ctx, g, mod, mod)


def _normmod_cm_body(x_ref, g_ref, sh_ref, sc_ref, o_ref, *, ncol):
    g = g_ref[...]
    sh = sh_ref[0:1, :]
    sc = sc_ref[0:1, :]
    for ci in range(ncol):
        o_ref[ci] = _norm_rows(x_ref[:, ci, :], g, sh, sc).astype(bf16)


def _normmod_colmajor(x3, g, mod, shift_blk, scale_blk, rb):
    r, w, d = x3.shape
    ncol = 8
    return pl.pallas_call(
        functools.partial(_normmod_cm_body, ncol=ncol),
        out_shape=jax.ShapeDtypeStruct((w, r, d), bf16),
        grid=(w // ncol, r // rb),
        in_specs=[pl.BlockSpec((rb, ncol, d), lambda c, j: (j, c, 0)),
                  pl.BlockSpec((1, d), lambda c, j: (0, 0)),
                  pl.BlockSpec((8, d), lambda c, j: (0, shift_blk)),
                  pl.BlockSpec((8, d), lambda c, j: (0, scale_blk))],
        out_specs=pl.BlockSpec((ncol, rb, d), lambda c, j: (c, j, 0)),
        compiler_params=_params(2 * rb * ncol * d * 4, semantics=("arbitrary", "arbitrary")),
        name="normmod_colmajor",
    )(x3, g, mod, mod)


def _mm_body(a_ref, b_ref, *rest, nt, epilogue):
    acc = (_bdot_nt if nt else _bdot)(a_ref[...], b_ref[...])
    o_ref = rest[-1]
    o_ref[...] = epilogue(acc, *rest[:-1]).astype(o_ref.dtype)


def _matmul(a, b, *, m, tm, tn, nt=False, a_off=0, extras=(), epilogue=None,
            out_dtype=f32, name="matmul"):
    k = a.shape[1]
    n = b.shape[0] if nt else b.shape[1]
    tn = min(tn, n)
    if epilogue is None:
        epilogue = lambda acc: acc
    b_spec = (pl.BlockSpec((tn, k), lambda i, j: (j, 0)) if nt
              else pl.BlockSpec((k, tn), lambda i, j: (0, j)))
    extra_bytes = sum(jnp.dtype(e.dtype).itemsize * math.prod(s.block_shape) for e, s in extras)
    return pl.pallas_call(
        functools.partial(_mm_body, nt=nt, epilogue=epilogue),
        out_shape=jax.ShapeDtypeStruct((m, n), out_dtype),
        grid=(m // tm, n // tn),
        in_specs=[pl.BlockSpec((tm, k), lambda i, j: (i + a_off, 0)), b_spec] + [s for _, s in extras],
        out_specs=pl.BlockSpec((tm, tn), lambda i, j: (i, j)),
        compiler_params=_params(tm * k * 2 + k * tn * 2 + tm * tn * 4 + extra_bytes,
                                semantics=("arbitrary", "arbitrary")),
        name=name,
    )(a, b, *[e for e, _ in extras])


def _gate_body(ga_ref, gb_ref, alog_ref, col_ref, row_ref, *, n_a, mh, gh):
    o_f, o_b, o_a = 2 * mh, 4 * mh, 4 * mh + 2 * gh
    is_b = pl.program_id(0) >= n_a
    g = jnp.where(is_b, gb_ref[...], ga_ref[...])
    n = g.shape[0]
    lane = lax.broadcasted_iota(i32, (n, LANES), 1)
    lf = -_softplus(-g)
    gl = -jnp.exp(alog_ref[...]) * _softplus(g)
    src = jnp.where(lane < o_b, lf, gl)
    rid = lax.broadcasted_iota(i32, (n, n), 0)
    cid = lax.broadcasted_iota(i32, (n, n), 1)
    tri_f = jnp.where(cid <= rid, 1.0, 0.0).astype(bf16)
    tri_b = jnp.where(cid >= rid, 1.0, 0.0).astype(bf16)
    parts = _split3(src)
    cf = sum(jnp.dot(tri_f, p, preferred_element_type=f32) for p in parts)
    cb = sum(jnp.dot(tri_b, p, preferred_element_type=f32) for p in parts)
    fwd = ((lane >= o_f) & (lane < o_f + mh)) | ((lane >= o_a) & (lane < o_a + gh))
    bwd = ((lane >= o_f + mh) & (lane < o_b)) | ((lane >= o_a + gh) & (lane < o_a + 2 * gh))
    beta = (lane >= o_b) & (lane < o_a)
    out = jnp.where(lane < o_f, g,
                    jnp.where(fwd, cf, jnp.where(bwd, cb, jnp.where(beta, _sigmoid(g), 0.0))))
    col_ref[...] = out
    row_ref[0] = out.T


def _gate_prep(g_a, n_a, g_b, b_off, n_b, alog_row, mh, gh):
    nc = n_a + n_b
    return pl.pallas_call(
        functools.partial(_gate_body, n_a=n_a, mh=mh, gh=gh),
        out_shape=(jax.ShapeDtypeStruct((nc * CHUNK, LANES), f32),
                   jax.ShapeDtypeStruct((nc, LANES, CHUNK), f32)),
        grid=(nc,),
        in_specs=[pl.BlockSpec((CHUNK, LANES), lambda i: (jnp.minimum(i, n_a - 1), 0)),
                  pl.BlockSpec((CHUNK, LANES), lambda i: (jnp.maximum(i - n_a, 0) + b_off, 0)),
                  pl.BlockSpec((1, LANES), lambda i: (0, 0))],
        out_specs=(pl.BlockSpec((CHUNK, LANES), lambda i: (i, 0)),
                   pl.BlockSpec((1, LANES, CHUNK), lambda i: (i, 0, 0))),
        compiler_params=_params(4 * CHUNK * LANES * 4, semantics=("arbitrary",)),
        name="gate_prep",
    )(g_a, g_b, alog_row)


def _fwd_chunk(j, n_lat, n_ctx):
    return jnp.where(j < n_ctx, n_lat + j, j - n_ctx)


def _bwd_chunk(j, n_lat, n_ctx):
    return n_lat + n_ctx - 1 - j


def _tri_masks(n):
    rid = lax.broadcasted_iota(i32, (n, n), 0)
    cid = lax.broadcasted_iota(i32, (n, n), 1)
    return cid <= rid, cid >= rid, cid < rid, cid > rid


def _mlstm_body(qf_ref, kf_ref, vf_ref, cf_ref, rf_ref, qb_ref, kb_ref, vb_ref, cb_ref, rb_ref,
                hf_ref, hb_ref, c_scr, n_scr, m_scr, *, mh, dh):
    @pl.when(pl.program_id(0) == 0)
    def _():
        c_scr[...] = jnp.zeros_like(c_scr)
        n_scr[...] = jnp.zeros_like(n_scr)
        m_scr[...] = jnp.full_like(m_scr, NEG)

    n = CHUNK
    incl_f, incl_b, _, _ = _tri_masks(n)
    scale = dh ** -0.5
    dirs = ((0, qf_ref, kf_ref, vf_ref, cf_ref, rf_ref, hf_ref, incl_f),
            (1, qb_ref, kb_ref, vb_ref, cb_ref, rb_ref, hb_ref, incl_b))
    chains = []
    for d, q_ref, k_ref, v_ref, col_ref, row_ref, h_ref, incl in dirs:
        for h in range(mh):
            ch = d * mh + h
            li, lb = d * mh + h, 2 * mh + d * mh + h
            sl = slice(h * dh, (h + 1) * dh)
            brow = row_ref[0, lb:lb + 1, :]
            chains.append(dict(
                ch=ch, sl=sl, h_ref=h_ref, incl=incl,
                q=q_ref[:, sl] * scale, k=k_ref[:, sl], v=v_ref[:, sl],
                icol=col_ref[:, li:li + 1], bcol=col_ref[:, lb:lb + 1],
                irow=row_ref[0, li:li + 1, :], brow=brow,
                btot=brow[:, n - 1:n] if d == 0 else brow[:, 0:1],
                m_st=m_scr[ch, :, 0:1], c_st=c_scr[ch], n_st=n_scr[ch]))
    for c in chains:
        dmat = jnp.where(c["incl"], c["bcol"] - c["brow"] + c["irow"], NEG)
        m_inter = c["bcol"] + c["m_st"]
        c["m_row"] = jnp.maximum(m_inter, jnp.max(dmat, axis=1, keepdims=True))
        c["p"] = jnp.exp(dmat - c["m_row"])
        c["w_inter"] = jnp.exp(m_inter - c["m_row"])
    for c in chains:
        c["s"] = _bdot_nt(c["q"], c["k"]) * c["p"]
        c["qc"] = _bdot(c["q"], c["c_st"])
    for c in chains:
        num = c["w_inter"] * c["qc"] + _bdot(c["s"], c["v"])
        den = (c["w_inter"] * jnp.sum(c["q"] * c["n_st"], axis=1, keepdims=True)
               + jnp.sum(c["s"], axis=1, keepdims=True))
        c["h_ref"][:, c["sl"]] = num / jnp.maximum(jnp.abs(den), jnp.exp(-c["m_row"]))
    for c in chains:
        ch = c["ch"]
        wlog = c["btot"] - c["bcol"] + c["icol"]
        m_new = jnp.maximum(c["btot"] + c["m_st"], jnp.max(wlog, axis=0, keepdims=True))
        a_st = jnp.exp(c["btot"] + c["m_st"] - m_new)
        kw = c["k"] * jnp.exp(wlog - m_new)
        c_scr[ch] = a_st * c["c_st"] + _bdot_tn(kw, c["v"])
        n_scr[ch] = a_st * c["n_st"] + jnp.sum(kw, axis=0, keepdims=True)
        m_scr[ch] = jnp.broadcast_to(m_new, (1, LANES))


def _mlstm_scan(pm, gcol, grow, n_lat, n_ctx, mh, dh):
    rows = pm.shape[0]
    mlw = mh * dh
    nc = n_lat + n_ctx
    fc = functools.partial(_fwd_chunk, n_lat=n_lat, n_ctx=n_ctx)
    bc = functools.partial(_bwd_chunk, n_lat=n_lat, n_ctx=n_ctx)

    def specs(cm):
        return [pl.BlockSpec((CHUNK, mlw), lambda j, p=p: (cm(j), p)) for p in range(3)] + [
            pl.BlockSpec((CHUNK, LANES), lambda j: (cm(j), 0)),
            pl.BlockSpec((1, LANES, CHUNK), lambda j: (cm(j), 0, 0))]

    blk = 2 * (3 * CHUNK * mlw * 4 + 2 * CHUNK * LANES * 4 + CHUNK * mlw * 4)
    scr = 2 * mh * (dh * dh + 8 * dh + 8 * LANES) * 4
    return pl.pallas_call(
        functools.partial(_mlstm_body, mh=mh, dh=dh),
        out_shape=(jax.ShapeDtypeStruct((rows, mlw), f32), jax.ShapeDtypeStruct((rows, mlw), f32)),
        grid=(nc,),
        in_specs=specs(fc) + specs(bc),
        out_specs=(pl.BlockSpec((CHUNK, mlw), lambda j: (fc(j), 0)),
                   pl.BlockSpec((CHUNK, mlw), lambda j: (bc(j), 0))),
        scratch_shapes=[pltpu.VMEM((2 * mh, dh, dh), f32),
                        pltpu.VMEM((2 * mh, 1, dh), f32),
                        pltpu.VMEM((2 * mh, 1, LANES), f32)],
        compiler_params=_params(blk, scr, semantics=("arbitrary",)),
        name="mlstm_scan",
    )(pm, pm, pm, gcol, grow, pm, pm, pm, gcol, grow)


def _gdn_conv_body(l_ref, c_ref, p_ref, n_ref, w_ref, b_ref, o_ref, ext, *, n_lat, kw, gdw, dk):
    b = pl.program_id(0)
    cw = o_ref.shape[1]
    rb = o_ref.shape[0]
    is_ctx = b >= n_lat
    half = kw // 2
    ext[0:8, :] = jnp.where((b == 0) | is_ctx, 0.0, p_ref[...])
    ext[8:8 + rb, :] = jnp.where(is_ctx, c_ref[...], l_ref[...])
    ext[8 + rb:16 + rb, :] = jnp.where(b >= n_lat - 1, 0.0, n_ref[...])
    y = jnp.broadcast_to(b_ref[...], (rb, cw))
    for j in range(kw):
        y = y + ext[pl.ds(8 - half + j, rb), :] * w_ref[j:j + 1, :]
    y = _silu(y)
    part = (pl.program_id(1) * cw) // gdw
    qk_scale = jnp.where(part == 0, dk ** -0.5, 1.0)
    for hh in range(cw // dk):
        sl = slice(hh * dk, (hh + 1) * dk)
        t = y[:, sl]
        tn = t * lax.rsqrt(jnp.sum(t * t, axis=-1, keepdims=True) + EPS) * qk_scale
        o_ref[:, sl] = jnp.where(part == 2, t, tn)


def _gdn_conv(pg_lat, pg_ctx, conv_w, conv_b, gdw, dk):
    t = pg_lat.shape[0]
    rb = pg_ctx.shape[0]
    n_lat = t // rb
    kw = conv_w.shape[0]
    cw = min(512, gdw)
    nh8 = rb // 8
    return pl.pallas_call(
        functools.partial(_gdn_conv_body, n_lat=n_lat, kw=kw, gdw=gdw, dk=dk),
        out_shape=jax.ShapeDtypeStruct((t + rb, 3 * gdw), f32),
        grid=(n_lat + 1, 3 * gdw // cw),
        in_specs=[pl.BlockSpec((rb, cw), lambda b, c: (jnp.minimum(b, n_lat - 1), c)),
                  pl.BlockSpec((rb, cw), lambda b, c: (0, c)),
                  pl.BlockSpec((8, cw), lambda b, c: (jnp.clip(b * nh8 - 1, 0, n_lat * nh8 - 1), c)),
                  pl.BlockSpec((8, cw), lambda b, c: (jnp.minimum((b + 1) * nh8, n_lat * nh8 - 1), c)),
                  pl.BlockSpec((kw, cw), lambda b, c: (0, c)),
                  pl.BlockSpec((1, cw), lambda b, c: (0, c))],
        out_specs=pl.BlockSpec((rb, cw), lambda b, c: (b, c)),
        scratch_shapes=[pltpu.VMEM((rb + 16, cw), f32)],
        compiler_params=_params(4 * rb * cw * 4, (rb + 16) * cw * 4, semantics=("arbitrary", "arbitrary")),
        name="gdn_conv",
    )(pg_lat, pg_ctx, pg_lat, pg_lat, conv_w, conv_b)


def _gdn_body(qf_ref, kf_ref, vf_ref, cf_ref, rf_ref, qb_ref, kb_ref, vb_ref, cb_ref, rb_ref,
              of_ref, ob_ref, s_scr, *, mh, gh, dk):
    @pl.when(pl.program_id(0) == 0)
    def _():
        s_scr[...] = jnp.zeros_like(s_scr)

    n = CHUNK
    incl_f, incl_b, strict_f, strict_b = _tri_masks(n)
    rid = lax.broadcasted_iota(i32, (n, n), 0)
    cid = lax.broadcasted_iota(i32, (n, n), 1)
    o_b, o_a = 4 * mh, 4 * mh + 2 * gh
    dirs = ((0, qf_ref, kf_ref, vf_ref, cf_ref, rf_ref, of_ref, incl_f, strict_f),
            (1, qb_ref, kb_ref, vb_ref, cb_ref, rb_ref, ob_ref, incl_b, strict_b))
    chains = []
    for d, q_ref, k_ref, v_ref, col_ref, row_ref, o_ref, incl, strict in dirs:
        for h in range(gh):
            lbeta, lg = o_b + d * gh + h, o_a + d * gh + h
            sl = slice(h * dk, (h + 1) * dk)
            grow = row_ref[0, lg:lg + 1, :]
            chains.append(dict(
                ch=d * gh + h, sl=sl, o_ref=o_ref, incl=incl, strict=strict,
                q=q_ref[:, sl], k=k_ref[:, sl], v=v_ref[:, sl],
                beta=col_ref[:, lbeta:lbeta + 1], gcol=col_ref[:, lg:lg + 1], grow=grow,
                glast=grow[:, n - 1:n] if d == 0 else grow[:, 0:1]))
    for c in chains:
        c["decay"] = jnp.exp(jnp.where(c["incl"], c["gcol"] - c["grow"], NEG))
        c["kb"] = c["k"] * c["beta"]
        c["egc"] = jnp.exp(c["gcol"])
    for c in chains:
        c["a"] = jnp.where(c["strict"], _bdot_nt(c["kb"], c["k"]) * c["decay"], 0.0)
        c["attn"] = _bdot_nt(c["q"], c["k"]) * c["decay"]
    eye = jnp.where(rid == cid, 1.0, 0.0)
    for c in chains:
        c["t"] = eye - jnp.where((rid >> 1) == (cid >> 1), c["a"], 0.0)
    sb = 1
    while (1 << sb) < n:
        off = ((rid >> (sb + 1)) == (cid >> (sb + 1))) & ((rid >> sb) != (cid >> sb))
        for c in chains:
            c["x"] = _bdot(jnp.where(off, c["a"], 0.0), c["t"])
        for c in chains:
            c["t"] = c["t"] - _bdot(c["t"], c["x"])
        sb += 1
    for c in chains:
        c["w"] = _bdot(c["t"], c["kb"] * c["egc"])
        c["u"] = _bdot(c["t"], c["v"] * c["beta"])
        c["s"] = s_scr[c["ch"]]
    for c in chains:
        c["v_new"] = c["u"] - _bdot(c["w"], c["s"])
        c["o"] = _bdot(c["q"] * c["egc"], c["s"])
    for c in chains:
        c["o_ref"][:, c["sl"]] = c["o"] + _bdot(c["attn"], c["v_new"])
        kd = c["k"] * jnp.exp(c["glast"] - c["gcol"])
        s_scr[c["ch"]] = jnp.exp(c["glast"]) * c["s"] + _bdot_tn(kd, c["v_new"])


def _gdn_scan(qkv, gcol, grow, n_lat, n_ctx, mh, gh, dk):
    rows = qkv.shape[0]
    gdw = gh * dk
    nc = n_lat + n_ctx
    fc = functools.partial(_fwd_chunk, n_lat=n_lat, n_ctx=n_ctx)
    bc = functools.partial(_bwd_chunk, n_lat=n_lat, n_ctx=n_ctx)

    def specs(cm):
        return [pl.BlockSpec((CHUNK, gdw), lambda j, p=p: (cm(j), p)) for p in range(3)] + [
            pl.BlockSpec((CHUNK, LANES), lambda j: (cm(j), 0)),
            pl.BlockSpec((1, LANES, CHUNK), lambda j: (cm(j), 0, 0))]

    blk = 2 * (3 * CHUNK * gdw * 4 + 2 * CHUNK * LANES * 4 + CHUNK * gdw * 4)
    return pl.pallas_call(
        functools.partial(_gdn_body, mh=mh, gh=gh, dk=dk),
        out_shape=(jax.ShapeDtypeStruct((rows, gdw), f32), jax.ShapeDtypeStruct((rows, gdw), f32)),
        grid=(nc,),
        in_specs=specs(fc) + specs(bc),
        out_specs=(pl.BlockSpec((CHUNK, gdw), lambda j: (fc(j), 0)),
                   pl.BlockSpec((CHUNK, gdw), lambda j: (bc(j), 0))),
        scratch_shapes=[pltpu.VMEM((2 * gh, dk, dk), f32)],
        compiler_params=_params(blk, 2 * gh * dk * dk * 4, semantics=("arbitrary",)),
        name="gdn_scan",
    )(qkv, qkv, qkv, gcol, grow, qkv, qkv, qkv, gcol, grow)


def _head_rms(t, g):
    return t * lax.rsqrt(jnp.mean(t * t, axis=-1, keepdims=True) + EPS) * g


def _mix_ml_body(hf_ref, hb_ref, o_gate_ref, g_ref, o_ref, *, mh, dh):
    for h in range(mh):
        sl = slice(h * dh, (h + 1) * dh)
        y = _head_rms(hf_ref[:, sl] + hb_ref[:, sl], g_ref[:, sl])
        o_ref[:, sl] = (_sigmoid(o_gate_ref[:, sl]) * y).astype(bf16)


def _mix_ml(hf, hb, pm, g_row, t, mh, dh, tm):
    mlw = mh * dh
    return pl.pallas_call(
        functools.partial(_mix_ml_body, mh=mh, dh=dh),
        out_shape=jax.ShapeDtypeStruct((t, mlw), bf16),
        grid=(t // tm,),
        in_specs=[pl.BlockSpec((tm, mlw), lambda i: (i, 0)),
                  pl.BlockSpec((tm, mlw), lambda i: (i, 0)),
                  pl.BlockSpec((tm, mlw), lambda i: (i, 3)),
                  pl.BlockSpec((1, mlw), lambda i: (0, 0))],
        out_specs=pl.BlockSpec((tm, mlw), lambda i: (i, 0)),
        compiler_params=_params(4 * tm * mlw * 4, semantics=("arbitrary",)),
        name="mix_ml",
    )(hf, hb, pm, g_row)


def _mix_gd_body(hf_ref, hb_ref, z_ref, g_ref, o_ref, *, gh, dk, ncol):
    for ci in range(ncol):
        for h in range(gh):
            sl = slice(h * dk, (h + 1) * dk)
            y = _head_rms(hf_ref[ci, :, sl] + hb_ref[ci, :, sl], g_ref[:, sl])
            o_ref[:, ci, sl] = (_silu(z_ref[ci, :, sl]) * y).astype(bf16)


def _mix_gd(hf3, hb3, pg3, g_row, r, w, gh, dk, rb):
    gdw = gh * dk
    ncol = 16
    return pl.pallas_call(
        functools.partial(_mix_gd_body, gh=gh, dk=dk, ncol=ncol),
        out_shape=jax.ShapeDtypeStruct((r, w, gdw), bf16),
        grid=(w // ncol, r // rb),
        in_specs=[pl.BlockSpec((ncol, rb, gdw), lambda c, j: (c, j, 0)),
                  pl.BlockSpec((ncol, rb, gdw), lambda c, j: (c, j, 0)),
                  pl.BlockSpec((ncol, rb, gdw), lambda c, j: (c, j, 3)),
                  pl.BlockSpec((1, gdw), lambda c, j: (0, 0))],
        out_specs=pl.BlockSpec((rb, ncol, gdw), lambda c, j: (j, c, 0)),
        compiler_params=_params(4 * ncol * rb * gdw * 4, semantics=("arbitrary", "arbitrary")),
        name="mix_gd",
    )(hf3, hb3, pg3, g_row)


def _outproj_body(a1_ref, a2_ref, b1_ref, b2_ref, x_ref, gate_ref, o_ref):
    y = _bdot(a1_ref[...], b1_ref[...]) + _bdot(a2_ref[...], b2_ref[...])
    o_ref[...] = x_ref[...] + gate_ref[0:1, :] * y


def _outproj(a1, a2, w1, w2, x, mod, gate_blk, tm, tn):
    t, k1 = a1.shape
    k2 = a2.shape[1]
    d = x.shape[1]
    tn = min(tn, d)
    gb = gate_blk * (d // tn)
    return pl.pallas_call(
        _outproj_body,
        out_shape=jax.ShapeDtypeStruct((t, d), f32),
        grid=(t // tm, d // tn),
        in_specs=[pl.BlockSpec((tm, k1), lambda i, j: (i, 0)),
                  pl.BlockSpec((tm, k2), lambda i, j: (i, 0)),
                  pl.BlockSpec((k1, tn), lambda i, j: (0, j)),
                  pl.BlockSpec((k2, tn), lambda i, j: (0, j)),
                  pl.BlockSpec((tm, tn), lambda i, j: (i, j)),
                  pl.BlockSpec((8, tn), lambda i, j: (0, gb + j))],
        out_specs=pl.BlockSpec((tm, tn), lambda i, j: (i, j)),
        compiler_params=_params(tm * (k1 + k2) * 2 + (k1 + k2) * tn * 2 + 2 * tm * tn * 4,
                                semantics=("arbitrary", "arbitrary")),
        name="outproj",
    )(a1, a2, w1, w2, x, mod)


def _topk_rows(s, kk, n_rows):
    rid = lax.broadcasted_iota(i32, s.shape, 0)
    vals, idxs = [], []
    for _ in range(kk):
        m = jnp.max(s, axis=0, keepdims=True)
        idx = jnp.min(jnp.where(s == m, rid, n_rows), axis=0, keepdims=True)
        s = jnp.where(rid == idx, -jnp.inf, s)
        vals.append(m)
        idxs.append(idx)
    return vals, idxs


def _peer_topk_body(q_ref, keys_ref, e_ref, g_ref, *, ph, nk, half):
    kk = PEER_TOPK
    tb = q_ref.shape[0]
    e_rows, g_rows = [], []
    for h in range(ph):
        sv, si = [], []
        for p in range(2):
            grp = 2 * h + p
            qg = q_ref[:, grp * half:(grp + 1) * half]
            kg = keys_ref[grp]
            k_hi, k_mid, _ = _split3(kg)
            q_hi, q_mid, _ = _split3(qg)
            nt = lambda a, b: lax.dot_general(a, b, (((1,), (1,)), ((), ())), preferred_element_type=f32)
            s = nt(k_hi, q_hi) + (nt(k_hi, q_mid) + nt(k_mid, q_hi))
            vals, idxs = _topk_rows(s, kk, nk)
            sv.append(vals)
            si.append(idxs)
        pairs = [(a, b) for a in range(kk) for b in range(kk) if (a + 1) * (b + 1) <= kk]
        npad = -len(pairs) % 8
        cand = jnp.concatenate([sv[0][a] + sv[1][b] for a, b in pairs]
                               + [jnp.full((npad, tb), -jnp.inf, f32)], axis=0)
        eidx = jnp.concatenate([si[0][a] * nk + si[1][b] for a, b in pairs]
                               + [jnp.full((npad, tb), -1, i32)], axis=0)
        n_cand = len(pairs) + npad
        rid = lax.broadcasted_iota(i32, cand.shape, 0)
        fv, fe = [], []
        for _ in range(kk):
            m = jnp.max(cand, axis=0, keepdims=True)
            pos = jnp.min(jnp.where(cand == m, rid, n_cand), axis=0, keepdims=True)
            hit = rid == pos
            fe.append(jnp.max(jnp.where(hit, eidx, -1), axis=0, keepdims=True))
            cand = jnp.where(hit, -jnp.inf, cand)
            fv.append(m)
        fv = jnp.concatenate(fv, axis=0)
        ex = jnp.exp(fv - fv[0:1, :])
        g_rows.append(ex / jnp.sum(ex, axis=0, keepdims=True))
        e_rows.append(jnp.concatenate(fe, axis=0))
    e_ref[...] = jnp.concatenate(e_rows, axis=0).T
    g_ref[...] = jnp.concatenate(g_rows, axis=0).T


def _peer_topk(qp, keys, ph, nk, half, tb):
    t = qp.shape[0]
    hk = ph * PEER_TOPK
    return pl.pallas_call(
        functools.partial(_peer_topk_body, ph=ph, nk=nk, half=half),
        out_shape=(jax.ShapeDtypeStruct((t, hk), i32), jax.ShapeDtypeStruct((t, hk), f32)),
        grid=(t // tb,),
        in_specs=[pl.BlockSpec((tb, 2 * ph * half), lambda i: (i, 0)),
                  pl.BlockSpec((2 * ph, nk, half), lambda i: (0, 0, 0))],
        out_specs=(pl.BlockSpec((tb, hk), lambda i: (i, 0)), pl.BlockSpec((tb, hk), lambda i: (i, 0))),
        compiler_params=_params(tb * 2 * ph * half * 4 + 2 * ph * nk * half * 4 + 2 * tb * hk * 4,
                                semantics=("arbitrary",)),
        name="peer_topk",
    )(qp, keys)


def _peer_w_body(e_ref, g_ref, w_ref, *, nk):
    tb, hk = e_ref.shape
    shift = nk.bit_length() - 1
    sub = lax.broadcasted_iota(i32, (nk, hk), 0)

    def group(t8, carry):
        base = pl.multiple_of(t8 * 8, 8)
        e8 = e_ref[pl.ds(base, 8), :]
        g8 = g_ref[pl.ds(base, 8), :]
        ws = []
        for t in range(8):
            e = e8[t:t + 1, :]
            g = g8[t:t + 1, :]
            ei = lax.shift_right_logical(e, shift)
            ej = e & (nk - 1)
            g_hi = g.astype(bf16).astype(f32)
            pt = jnp.where(sub == ei, 1.0, 0.0).astype(bf16)
            hit = sub == ej
            qt = jnp.concatenate([jnp.where(hit, g_hi, 0.0).astype(bf16),
                                  jnp.where(hit, g - g_hi, 0.0).astype(bf16)], axis=1)
            ws.append(lax.dot_general(jnp.concatenate([pt, pt], axis=1), qt, (((1,), (1,)), ((), ())),
                                      preferred_element_type=f32))
        w_ref[:, pl.ds(base, 8), :] = jnp.stack(ws, axis=1)
        return carry

    lax.fori_loop(0, tb // 8, group, 0)


def _peer_weights(e, g, nk, tb):
    t, hk = e.shape
    return pl.pallas_call(
        functools.partial(_peer_w_body, nk=nk),
        out_shape=jax.ShapeDtypeStruct((nk, t, nk), f32),
        grid=(t // tb,),
        in_specs=[pl.BlockSpec((tb, hk), lambda i: (i, 0)), pl.BlockSpec((tb, hk), lambda i: (i, 0))],
        out_specs=pl.BlockSpec((nk, tb, nk), lambda i: (0, i, 0)),
        compiler_params=_params(tb * nk * nk * 4 + 2 * tb * hk * 4, semantics=("arbitrary",)),
        name="peer_weights",
    )(e, g)


def _gelu(x):
    return 0.5 * x * (1.0 + lax.erf(x * 0.7071067811865476))


def _peer_act_body(f_ref, u_ref, w_ref, o_ref, *, nk, gi):
    acc = _bdot_nt(f_ref[...], u_ref[...])
    for gg in range(gi):
        sl = slice(gg * nk, (gg + 1) * nk)
        o_ref[:, sl] = (w_ref[gg] * _gelu(acc[:, sl])).astype(bf16)


def _peer_act(f, u, w3, nk, tm):
    t, d = f.shape
    ne = u.shape[0]
    gi = 8
    tn = gi * nk
    return pl.pallas_call(
        functools.partial(_peer_act_body, nk=nk, gi=gi),
        out_shape=jax.ShapeDtypeStruct((t, ne), bf16),
        grid=(t // tm, ne // tn),
        in_specs=[pl.BlockSpec((tm, d), lambda i, j: (i, 0)),
                  pl.BlockSpec((tn, d), lambda i, j: (j, 0)),
                  pl.BlockSpec((gi, tm, nk), lambda i, j: (j, i, 0))],
        out_specs=pl.BlockSpec((tm, tn), lambda i, j: (i, j)),
        compiler_params=_params(tm * d * 2 + tn * d * 2 + tm * tn * 4 + tm * tn * 2 + tm * tn * 4,
                                semantics=("arbitrary", "arbitrary")),
        name="peer_act",
    )(f, u, w3)


def _peer_out_body(a_ref, v_ref, x_ref, gate_ref, o_ref, acc):
    kq = pl.program_id(2)

    @pl.when(kq == 0)
    def _():
        acc[...] = jnp.zeros_like(acc)

    acc[...] += jnp.dot(a_ref[...], v_ref[...], preferred_element_type=f32)

    @pl.when(kq == pl.num_programs(2) - 1)
    def _():
        o_ref[...] = x_ref[...] + gate_ref[0:1, :] * acc[...]


def _peer_out(a, v, x, mod, gate_blk, tm, tn, tk):
    t, ne = a.shape
    d = v.shape[1]
    tn = min(tn, d)
    tk = min(tk, ne)
    gb = gate_blk * (d // tn)
    return pl.pallas_call(
        _peer_out_body,
        out_shape=jax.ShapeDtypeStruct((t, d), f32),
        grid=(t // tm, d // tn, ne // tk),
        in_specs=[pl.BlockSpec((tm, tk), lambda i, j, k: (i, k)),
                  pl.BlockSpec((tk, tn), lambda i, j, k: (k, j)),
                  pl.BlockSpec((tm, tn), lambda i, j, k: (i, j)),
                  pl.BlockSpec((8, tn), lambda i, j, k: (0, gb + j))],
        out_specs=pl.BlockSpec((tm, tn), lambda i, j, k: (i, j)),
        scratch_shapes=[pltpu.VMEM((tm, tn), f32)],
        compiler_params=_params(tm * tk * 2 + tk * tn * 2 + 2 * tm * tn * 4, tm * tn * 4,
                                semantics=("arbitrary", "arbitrary", "arbitrary")),
        name="peer_out",
    )(a, v, x, mod)


def _final_norm_body(x_ref, g_ref, o_ref):
    x = x_ref[...]
    o_ref[...] = x * lax.rsqrt(jnp.mean(x * x, axis=-1, keepdims=True) + EPS) * g_ref[...]


def _final_norm(x, g, tm):
    t, d = x.shape
    return pl.pallas_call(
        _final_norm_body,
        out_shape=jax.ShapeDtypeStruct((t, d), f32),
        grid=(t // tm,),
        in_specs=[pl.BlockSpec((tm, d), lambda i: (i, 0)), pl.BlockSpec((1, d), lambda i: (0, 0))],
        out_specs=pl.BlockSpec((tm, d), lambda i: (i, 0)),
        compiler_params=_params(2 * tm * d * 4, semantics=("arbitrary",)),
        name="final_norm",
    )(x, g)


def _pick(n, pref):
    t = min(pref, n)
    while n % t:
        t -= 8
    return t


def kernel(x, c, ctx, c_ctx, w_mod, b_mod, g_mix, w_in, b_gate, gd_conv_w, gd_conv_b, gd_a_log,
           ml_norm_g, gd_norm_g, w_out, g_ffn, peer_wq, peer_keys, peer_u, peer_v, g_final):
    return _block(x, c, ctx, c_ctx, w_mod, b_mod, g_mix, w_in, b_gate, gd_conv_w, gd_conv_b, gd_a_log,
                  ml_norm_g, gd_norm_g, w_out, g_ffn, peer_wq, peer_keys, peer_u, peer_v, g_final)[0]


def _block(x, c, ctx, c_ctx, w_mod, b_mod, g_mix, w_in, b_gate, gd_conv_w, gd_conv_b, gd_a_log,
           ml_norm_g, gd_norm_g, w_out, g_ffn, peer_wq, peer_keys, peer_u, peer_v, g_final):
    assert w_mod.shape[0] == 1 and x.shape[0] == 1, "one layer, one batch element"
    _, t, d = x.shape
    tc = ctx.shape[1]
    _, mh, dh = ml_norm_g.shape
    _, gh, dk = gd_norm_g.shape
    mlw, gdw = mh * dh, gh * dk
    _, ph, _, nk, half = peer_keys.shape
    ne = peer_u.shape[1]
    w = GRID_W
    r = t // w
    assert r == tc and t % CHUNK == 0 and tc % CHUNK == 0 and r % CHUNK == 0
    assert ne == nk * nk and nk & (nk - 1) == 0 and 4 * mh + 4 * gh <= LANES
    n_lat, n_ctx = t // CHUNK, tc // CHUNK

    x2 = x.reshape(t, d)
    ctx2 = ctx.reshape(tc, d)

    s8 = jnp.zeros((8, d), f32).at[0].set(c[0]).at[1].set(c_ctx)
    mod = _mod_vectors(s8, w_mod[0], b_mod)

    h_rm = _normmod(x2, ctx2, g_mix, mod, 0, 1, tm=tc)
    h_cm = _normmod_colmajor(x2.reshape(r, w, d), g_mix[0:1], mod, 0, 1, rb=_pick(r, 64)).reshape(t, d)

    main = 4 * mlw + 4 * gdw
    w_in0 = w_in[0]
    w_ml = w_in0[:, :4 * mlw].astype(bf16)
    w_gd = w_in0[:, 4 * mlw:main].astype(bf16)
    ng = w_in0.shape[1] - main
    w_gate = jnp.pad(w_in0[:, main:], ((0, 0), (0, LANES - ng))).astype(bf16)
    b_gate_row = jnp.pad(b_gate, ((0, 0), (0, LANES - ng)))
    rows = t + tc
    tm_all = _pick(rows, 1280)
    tm_lat = _pick(t, 1024)
    pm = _matmul(h_rm, w_ml, m=rows, tm=tm_all, tn=512, name="proj_ml")
    pg_lat = _matmul(h_cm, w_gd, m=t, tm=tm_lat, tn=512, name="proj_gd_lat")
    pg_ctx = _matmul(h_rm, w_gd, m=tc, tm=tc, tn=512, a_off=t // tc, name="proj_gd_ctx")
    bias_spec = pl.BlockSpec((1, LANES), lambda i, j: (0, 0))
    add_bias = lambda acc, b_ref: acc + b_ref[...]
    g_rm = _matmul(h_rm, w_gate, m=rows, tm=tm_all, tn=LANES, extras=[(b_gate_row, bias_spec)],
                   epilogue=add_bias, name="proj_gate_rm")
    g_cm = _matmul(h_cm, w_gate, m=t, tm=tm_lat, tn=LANES, extras=[(b_gate_row, bias_spec)],
                   epilogue=add_bias, name="proj_gate_cm")

    o_a = 4 * mh + 2 * gh
    alog_row = jnp.zeros((1, LANES), f32).at[0, o_a:o_a + 2 * gh].set(gd_a_log.reshape(-1))
    ml_col, ml_row = _gate_prep(g_rm, n_lat, g_rm, n_lat, n_ctx, alog_row, mh, gh)
    gd_col, gd_row = _gate_prep(g_cm, n_lat, g_rm, n_lat, n_ctx, alog_row, mh, gh)

    hm_f, hm_b = _mlstm_scan(pm, ml_col, ml_row, n_lat, n_ctx, mh, dh)

    qkv = _gdn_conv(pg_lat, pg_ctx, gd_conv_w[0], gd_conv_b, gdw, dk)
    hg_f, hg_b = _gdn_scan(qkv, gd_col, gd_row, n_lat, n_ctx, mh, gh, dk)

    mix_ml = _mix_ml(hm_f, hm_b, pm, ml_norm_g.reshape(1, mlw), t, mh, dh, tm=_pick(t, 256))
    ncols = rows // r
    mix_gd = _mix_gd(hg_f.reshape(ncols, r, gdw), hg_b.reshape(ncols, r, gdw),
                     pg_lat.reshape(w, r, 4 * gdw), gd_norm_g.reshape(1, gdw), r, w, gh, dk,
                     rb=_pick(r, 32)).reshape(t, gdw)
    w_out0 = w_out[0].astype(bf16)
    x1 = _outproj(mix_ml, mix_gd, w_out0[:mlw], w_out0[mlw:], x2, mod, 2, tm=tm_lat, tn=512)

    f_in = _normmod(x1, None, g_ffn, mod, 3, 4, tm=_pick(t, 256))
    qp = _matmul(f_in, peer_wq[0].astype(bf16), m=t, tm=tm_lat, tn=512, name="peer_query")
    e_idx, gates = _peer_topk(qp, peer_keys[0].reshape(2 * ph, nk, half), ph, nk, half, tb=_pick(t, 256))
    w3 = _peer_weights(e_idx, gates, nk, tb=_pick(t, 128))
    act = _peer_act(f_in, peer_u[0].astype(bf16), w3, nk, tm=_pick(t, 512))
    x2_out = _peer_out(act, peer_v[0].astype(bf16), x1, mod, 5, tm=_pick(t, 512), tn=2048, tk=1024)

    out = _final_norm(x2_out, g_final.reshape(1, d), tm=_pick(t, 256))
    aux = dict(mod=mod, h_rm=h_rm, h_cm=h_cm, pm=pm, pg_lat=pg_lat, pg_ctx=pg_ctx, g_rm=g_rm, g_cm=g_cm,
               ml_col=ml_col, ml_row=ml_row, gd_col=gd_col, gd_row=gd_row, hm_f=hm_f, hm_b=hm_b, qkv=qkv,
               hg_f=hg_f, hg_b=hg_b, mix_ml=mix_ml, mix_gd=mix_gd, x1=x1, f_in=f_in, qp=qp, e_idx=e_idx,
               gates=gates, w3=w3, act=act, x2_out=x2_out)
    return out.reshape(1, t, d), aux
```

```python
import functools
import math

import jax
import jax.numpy as jnp
from jax import lax
from jax.experimental import pallas as pl
from jax.experimental.pallas import tpu as pltpu

f32 = jnp.float32
bf16 = jnp.bfloat16
i32 = jnp.int32

GRID_W = 64
PEER_TOPK = 16
EPS = 1e-6
NEG = -1e30
CHUNK = 128
LANES = 128
VMEM_CAP = 56 * 1024 * 1024


def _params(block_bytes, scratch_bytes=0, semantics=None):
    est = 2 * block_bytes + scratch_bytes + 16 * 1024 * 1024
    return pltpu.CompilerParams(
        dimension_semantics=semantics,
        vmem_limit_bytes=int(min(max(est, 32 * 1024 * 1024), VMEM_CAP)))


def _bdot(a, b):
    return jnp.dot(a.astype(bf16), b.astype(bf16), preferred_element_type=f32)


def _bdot_nt(a, b):
    return lax.dot_general(a.astype(bf16), b.astype(bf16), (((1,), (1,)), ((), ())),
                           preferred_element_type=f32)


def _bdot_tn(a, b):
    return lax.dot_general(a.astype(bf16), b.astype(bf16), (((0,), (0,)), ((), ())),
                           preferred_element_type=f32)


def _split3(x):
    hi = x.astype(bf16)
    r = x - hi.astype(f32)
    mid = r.astype(bf16)
    lo = (r - mid.astype(f32)).astype(bf16)
    return hi, mid, lo


def _softplus(x):
    return jnp.maximum(x, 0.0) + jnp.log1p(jnp.exp(-jnp.abs(x)))


def _sigmoid(x):
    return 1.0 / (1.0 + jnp.exp(-x))


def _silu(x):
    return x * _sigmoid(x)


def _mod_body(s_ref, w_ref, b_ref, o_ref):
    o_ref[...] = _bdot(_silu(s_ref[...]), w_ref[...]) + b_ref[...]


def _mod_vectors(s8, w_mod, b_mod):
    d, n = w_mod.shape
    tn = min(512, n)
    return pl.pallas_call(
        _mod_body,
        out_shape=jax.ShapeDtypeStruct((8, n), f32),
        grid=(n // tn,),
        in_specs=[pl.BlockSpec((8, d), lambda j: (0, 0)),
                  pl.BlockSpec((d, tn), lambda j: (0, j)),
                  pl.BlockSpec((1, tn), lambda j: (0, j))],
        out_specs=pl.BlockSpec((8, tn), lambda j: (0, j)),
        compiler_params=_params(d * tn * 4 + 8 * d * 4, semantics=("arbitrary",)),
        name="mod_vectors",
    )(s8, w_mod, b_mod)


def _norm_rows(x, g, shift, scale):
    y = x * lax.rsqrt(jnp.mean(x * x, axis=-1, keepdims=True) + EPS)
    return (y * g) * (1.0 + scale) + shift


def _normmod_body(x_ref, c_ref, g_ref, sh_ref, sc_ref, o_ref, *, n_lat):
    is_ctx = pl.program_id(0) >= n_lat
    x = jnp.where(is_ctx, c_ref[...], x_ref[...])
    sh = jnp.where(is_ctx, sh_ref[1:2, :], sh_ref[0:1, :])
    sc = jnp.where(is_ctx, sc_ref[1:2, :], sc_ref[0:1, :])
    o_ref[...] = _norm_rows(x, g_ref[...], sh, sc).astype(bf16)


def _normmod(x, ctx, g, mod, shift_blk, scale_blk, tm):
    t, d = x.shape
    n_lat = t // tm
    if ctx is None:
        ctx = x
        n_ctx = 0
    else:
        n_ctx = ctx.shape[0] // tm
    rows = (n_lat + n_ctx) * tm
    return pl.pallas_call(
        functools.partial(_normmod_body, n_lat=n_lat),
        out_shape=jax.ShapeDtypeStruct((rows, d), bf16),
        grid=(n_lat + n_ctx,),
        in_specs=[pl.BlockSpec((tm, d), lambda i: (jnp.minimum(i, n_lat - 1), 0)),
                  pl.BlockSpec((tm, d), lambda i: (jnp.maximum(i - n_lat, 0), 0)),
                  pl.BlockSpec((1, d), lambda i: (0, 0)),
                  pl.BlockSpec((8, d), lambda i: (0, shift_blk)),
                  pl.BlockSpec((8, d), lambda i: (0, scale_blk))],
        out_specs=pl.BlockSpec((tm, d), lambda i: (i, 0)),
        compiler_params=_params(3 * tm * d * 4, semantics=("arbitrary",)),
        name="normmod",
    )(x, ctx, g, mod, mod)


def _normmod_cm_body(x_ref, g_ref, sh_ref, sc_ref, o_ref, *, ncol):
    x = jnp.swapaxes(x_ref[...], 0, 1)
    o_ref[...] = _norm_rows(x, g_ref[...], sh_ref[0:1, :], sc_ref[0:1, :]).astype(bf16)


def _normmod_colmajor(x3, g, mod, shift_blk, scale_blk, rb):
    r, w, d = x3.shape
    ncol = 8
    return pl.pallas_call(
        functools.partial(_normmod_cm_body, ncol=ncol),
        out_shape=jax.ShapeDtypeStruct((w, r, d), bf16),
        grid=(w // ncol, r // rb),
        in_specs=[pl.BlockSpec((rb, ncol, d), lambda c, j: (j, c, 0)),
                  pl.BlockSpec((1, d), lambda c, j: (0, 0)),
                  pl.BlockSpec((8, d), lambda c, j: (0, shift_blk)),
                  pl.BlockSpec((8, d), lambda c, j: (0, scale_blk))],
        out_specs=pl.BlockSpec((ncol, rb, d), lambda c, j: (c, j, 0)),
        compiler_params=_params(2 * rb * ncol * d * 4, semantics=("arbitrary", "arbitrary")),
        name="normmod_colmajor",
    )(x3, g, mod, mod)


def _mm_body(a_ref, b_ref, *rest, nt, epilogue):
    acc = (_bdot_nt if nt else _bdot)(a_ref[...], b_ref[...])
    o_ref = rest[-1]
    o_ref[...] = epilogue(acc, *rest[:-1]).astype(o_ref.dtype)


def _matmul(a, b, *, m, tm, tn, nt=False, a_off=0, extras=(), epilogue=None,
            out_dtype=f32, name="matmul"):
    k = a.shape[1]
    n = b.shape[0] if nt else b.shape[1]
    tn = min(tn, n)
    if epilogue is None:
        epilogue = lambda acc: acc
    b_spec = (pl.BlockSpec((tn, k), lambda i, j: (j, 0)) if nt
              else pl.BlockSpec((k, tn), lambda i, j: (0, j)))
    extra_bytes = sum(jnp.dtype(e.dtype).itemsize * math.prod(s.block_shape) for e, s in extras)
    return pl.pallas_call(
        functools.partial(_mm_body, nt=nt, epilogue=epilogue),
        out_shape=jax.ShapeDtypeStruct((m, n), out_dtype),
        grid=(m // tm, n // tn),
        in_specs=[pl.BlockSpec((tm, k), lambda i, j: (i + a_off, 0)), b_spec] + [s for _, s in extras],
        out_specs=pl.BlockSpec((tm, tn), lambda i, j: (i, j)),
        compiler_params=_params(tm * k * 2 + k * tn * 2 + tm * tn * 4 + extra_bytes,
                                semantics=("arbitrary", "arbitrary")),
        name=name,
    )(a, b, *[e for e, _ in extras])


def _gate_body(ga_ref, gb_ref, alog_ref, col_ref, row_ref, *, n_a, mh, gh):
    o_f, o_b, o_a = 2 * mh, 4 * mh, 4 * mh + 2 * gh
    is_b = pl.program_id(0) >= n_a
    g = jnp.where(is_b, gb_ref[...], ga_ref[...])
    n = g.shape[0]
    lane = lax.broadcasted_iota(i32, (n, LANES), 1)
    lf = -_softplus(-g)
    gl = -jnp.exp(alog_ref[...]) * _softplus(g)
    src = jnp.where(lane < o_b, lf, gl)
    rid = lax.broadcasted_iota(i32, (n, n), 0)
    cid = lax.broadcasted_iota(i32, (n, n), 1)
    tri_f = jnp.where(cid <= rid, 1.0, 0.0).astype(bf16)
    tri_b = jnp.where(cid >= rid, 1.0, 0.0).astype(bf16)
    parts = _split3(src)
    cf = sum(jnp.dot(tri_f, p, preferred_element_type=f32) for p in parts)
    cb = sum(jnp.dot(tri_b, p, preferred_element_type=f32) for p in parts)
    fwd = ((lane >= o_f) & (lane < o_f + mh)) | ((lane >= o_a) & (lane < o_a + gh))
    bwd = ((lane >= o_f + mh) & (lane < o_b)) | ((lane >= o_a + gh) & (lane < o_a + 2 * gh))
    beta = (lane >= o_b) & (lane < o_a)
    out = jnp.where(lane < o_f, g,
                    jnp.where(fwd, cf, jnp.where(bwd, cb, jnp.where(beta, _sigmoid(g), 0.0))))
    col_ref[...] = out
    row_ref[0] = out.T


def _gate_prep(g_a, n_a, g_b, b_off, n_b, alog_row, mh, gh):
    nc = n_a + n_b
    return pl.pallas_call(
        functools.partial(_gate_body, n_a=n_a, mh=mh, gh=gh),
        out_shape=(jax.ShapeDtypeStruct((nc * CHUNK, LANES), f32),
                   jax.ShapeDtypeStruct((nc, LANES, CHUNK), f32)),
        grid=(nc,),
        in_specs=[pl.BlockSpec((CHUNK, LANES), lambda i: (jnp.minimum(i, n_a - 1), 0)),
                  pl.BlockSpec((CHUNK, LANES), lambda i: (jnp.maximum(i - n_a, 0) + b_off, 0)),
                  pl.BlockSpec((1, LANES), lambda i: (0, 0))],
        out_specs=(pl.BlockSpec((CHUNK, LANES), lambda i: (i, 0)),
                   pl.BlockSpec((1, LANES, CHUNK), lambda i: (i, 0, 0))),
        compiler_params=_params(4 * CHUNK * LANES * 4, semantics=("arbitrary",)),
        name="gate_prep",
    )(g_a, g_b, alog_row)


def _fwd_chunk(j, n_lat, n_ctx):
    return jnp.where(j < n_ctx, n_lat + j, j - n_ctx)


def _bwd_chunk(j, n_lat, n_ctx):
    return n_lat + n_ctx - 1 - j


def _tri_masks(n):
    rid = lax.broadcasted_iota(i32, (n, n), 0)
    cid = lax.broadcasted_iota(i32, (n, n), 1)
    return cid <= rid, cid >= rid, cid < rid, cid > rid


def _mlstm_body(qf_ref, kf_ref, vf_ref, cf_ref, rf_ref, qb_ref, kb_ref, vb_ref, cb_ref, rb_ref,
                hf_ref, hb_ref, c_scr, n_scr, m_scr, *, mh, dh):
    @pl.when(pl.program_id(0) == 0)
    def _():
        c_scr[...] = jnp.zeros_like(c_scr)
        n_scr[...] = jnp.zeros_like(n_scr)
        m_scr[...] = jnp.full_like(m_scr, NEG)

    n = CHUNK
    incl_f, incl_b, _, _ = _tri_masks(n)
    scale = dh ** -0.5
    dirs = ((0, qf_ref, kf_ref, vf_ref, cf_ref, rf_ref, hf_ref, incl_f),
            (1, qb_ref, kb_ref, vb_ref, cb_ref, rb_ref, hb_ref, incl_b))
    chains = []
    for d, q_ref, k_ref, v_ref, col_ref, row_ref, h_ref, incl in dirs:
        for h in range(mh):
            ch = d * mh + h
            li, lb = d * mh + h, 2 * mh + d * mh + h
            sl = slice(h * dh, (h + 1) * dh)
            brow = row_ref[0, lb:lb + 1, :]
            chains.append(dict(
                ch=ch, sl=sl, h_ref=h_ref, incl=incl,
                q=q_ref[:, sl] * scale, k=k_ref[:, sl], v=v_ref[:, sl],
                icol=col_ref[:, li:li + 1], bcol=col_ref[:, lb:lb + 1],
                irow=row_ref[0, li:li + 1, :], brow=brow,
                btot=brow[:, n - 1:n] if d == 0 else brow[:, 0:1],
                m_st=m_scr[ch, :, 0:1], c_st=c_scr[ch], n_st=n_scr[ch]))
    for c in chains:
        dmat = jnp.where(c["incl"], c["bcol"] - c["brow"] + c["irow"], NEG)
        m_inter = c["bcol"] + c["m_st"]
        c["m_row"] = jnp.maximum(m_inter, jnp.max(dmat, axis=1, keepdims=True))
        c["p"] = jnp.exp(dmat - c["m_row"])
        c["w_inter"] = jnp.exp(m_inter - c["m_row"])
    for c in chains:
        c["s"] = _bdot_nt(c["q"], c["k"]) * c["p"]
        c["qc"] = _bdot(c["q"], c["c_st"])
    for c in chains:
        num = c["w_inter"] * c["qc"] + _bdot(c["s"], c["v"])
        den = (c["w_inter"] * jnp.sum(c["q"] * c["n_st"], axis=1, keepdims=True)
               + jnp.sum(c["s"], axis=1, keepdims=True))
        c["h_ref"][:, c["sl"]] = num / jnp.maximum(jnp.abs(den), jnp.exp(-c["m_row"]))
    eye = jnp.where(lax.broadcasted_iota(i32, (dh, dh), 0) == lax.broadcasted_iota(i32, (dh, dh), 1),
                    1.0, 0.0).astype(bf16)
    for c in chains:
        ch = c["ch"]
        wlog = c["btot"] - c["bcol"] + c["icol"]
        m_new = jnp.maximum(c["btot"] + c["m_st"], jnp.max(wlog, axis=0, keepdims=True))
        a_st = jnp.exp(c["btot"] + c["m_st"] - m_new)
        kw = c["k"] * jnp.exp(wlog - m_new)
        kw_t = _bdot_nt(eye, kw)
        c_scr[ch] = a_st * c["c_st"] + _bdot(kw_t, c["v"])
        n_scr[ch] = a_st * c["n_st"] + jnp.sum(kw, axis=0, keepdims=True)
        m_scr[ch] = jnp.broadcast_to(m_new, (1, LANES))


def _mlstm_scan(pm, gcol, grow, n_lat, n_ctx, mh, dh):
    rows = pm.shape[0]
    mlw = mh * dh
    nc = n_lat + n_ctx
    fc = functools.partial(_fwd_chunk, n_lat=n_lat, n_ctx=n_ctx)
    bc = functools.partial(_bwd_chunk, n_lat=n_lat, n_ctx=n_ctx)

    def specs(cm):
        return [pl.BlockSpec((CHUNK, mlw), lambda j, p=p: (cm(j), p)) for p in range(3)] + [
            pl.BlockSpec((CHUNK, LANES), lambda j: (cm(j), 0)),
            pl.BlockSpec((1, LANES, CHUNK), lambda j: (cm(j), 0, 0))]

    blk = 2 * (3 * CHUNK * mlw * 4 + 2 * CHUNK * LANES * 4 + CHUNK * mlw * 4)
    scr = 2 * mh * (dh * dh + 8 * dh + 8 * LANES) * 4
    return pl.pallas_call(
        functools.partial(_mlstm_body, mh=mh, dh=dh),
        out_shape=(jax.ShapeDtypeStruct((rows, mlw), f32), jax.ShapeDtypeStruct((rows, mlw), f32)),
        grid=(nc,),
        in_specs=specs(fc) + specs(bc),
        out_specs=(pl.BlockSpec((CHUNK, mlw), lambda j: (fc(j), 0)),
                   pl.BlockSpec((CHUNK, mlw), lambda j: (bc(j), 0))),
        scratch_shapes=[pltpu.VMEM((2 * mh, dh, dh), f32),
                        pltpu.VMEM((2 * mh, 1, dh), f32),
                        pltpu.VMEM((2 * mh, 1, LANES), f32)],
        compiler_params=_params(blk, scr, semantics=("arbitrary",)),
        name="mlstm_scan",
    )(pm, pm, pm, gcol, grow, pm, pm, pm, gcol, grow)


def _gdn_conv_body(l_ref, c_ref, p_ref, n_ref, w_ref, b_ref, o_ref, ext, *, n_lat, kw, gdw, dk):
    b = pl.program_id(0)
    cw = o_ref.shape[1]
    rb = o_ref.shape[0]
    is_ctx = b >= n_lat
    half = kw // 2
    ext[0:8, :] = jnp.where((b == 0) | is_ctx, 0.0, p_ref[...])
    ext[8:8 + rb, :] = jnp.where(is_ctx, c_ref[...], l_ref[...])
    ext[8 + rb:16 + rb, :] = jnp.where(b >= n_lat - 1, 0.0, n_ref[...])
    y = jnp.broadcast_to(b_ref[...], (rb, cw))
    for j in range(kw):
        y = y + ext[pl.ds(8 - half + j, rb), :] * w_ref[j:j + 1, :]
    y = _silu(y)
    part = (pl.program_id(1) * cw) // gdw
    qk_scale = jnp.where(part == 0, dk ** -0.5, 1.0)
    for hh in range(cw // dk):
        sl = slice(hh * dk, (hh + 1) * dk)
        t = y[:, sl]
        tn = t * lax.rsqrt(jnp.sum(t * t, axis=-1, keepdims=True) + EPS) * qk_scale
        o_ref[:, sl] = jnp.where(part == 2, t, tn)


def _gdn_conv(pg_lat, pg_ctx, conv_w, conv_b, gdw, dk):
    t = pg_lat.shape[0]
    rb = pg_ctx.shape[0]
    n_lat = t // rb
    kw = conv_w.shape[0]
    cw = gdw
    nh8 = rb // 8
    return pl.pallas_call(
        functools.partial(_gdn_conv_body, n_lat=n_lat, kw=kw, gdw=gdw, dk=dk),
        out_shape=jax.ShapeDtypeStruct((t + rb, 3 * gdw), f32),
        grid=(n_lat + 1, 3 * gdw // cw),
        in_specs=[pl.BlockSpec((rb, cw), lambda b, c: (jnp.minimum(b, n_lat - 1), c)),
                  pl.BlockSpec((rb, cw), lambda b, c: (0, c)),
                  pl.BlockSpec((8, cw), lambda b, c: (jnp.clip(b * nh8 - 1, 0, n_lat * nh8 - 1), c)),
                  pl.BlockSpec((8, cw), lambda b, c: (jnp.minimum((b + 1) * nh8, n_lat * nh8 - 1), c)),
                  pl.BlockSpec((kw, cw), lambda b, c: (0, c)),
                  pl.BlockSpec((1, cw), lambda b, c: (0, c))],
        out_specs=pl.BlockSpec((rb, cw), lambda b, c: (b, c)),
        scratch_shapes=[pltpu.VMEM((rb + 16, cw), f32)],
        compiler_params=_params(4 * rb * cw * 4, (rb + 16) * cw * 4, semantics=("arbitrary", "arbitrary")),
        name="gdn_conv",
    )(pg_lat, pg_ctx, pg_lat, pg_lat, conv_w, conv_b)


def _gdn_body(qf_ref, kf_ref, vf_ref, cf_ref, rf_ref, qb_ref, kb_ref, vb_ref, cb_ref, rb_ref,
              of_ref, ob_ref, s_scr, *, mh, gh, dk):
    @pl.when(pl.program_id(0) == 0)
    def _():
        s_scr[...] = jnp.zeros_like(s_scr)

    n = CHUNK
    incl_f, incl_b, strict_f, strict_b = _tri_masks(n)
    rid = lax.broadcasted_iota(i32, (n, n), 0)
    cid = lax.broadcasted_iota(i32, (n, n), 1)
    o_b, o_a = 4 * mh, 4 * mh + 2 * gh
    dirs = ((0, qf_ref, kf_ref, vf_ref, cf_ref, rf_ref, of_ref, incl_f, strict_f),
            (1, qb_ref, kb_ref, vb_ref, cb_ref, rb_ref, ob_ref, incl_b, strict_b))
    chains = []
    for d, q_ref, k_ref, v_ref, col_ref, row_ref, o_ref, incl, strict in dirs:
        for h in range(gh):
            lbeta, lg = o_b + d * gh + h, o_a + d * gh + h
            sl = slice(h * dk, (h + 1) * dk)
            grow = row_ref[0, lg:lg + 1, :]
            chains.append(dict(
                ch=d * gh + h, sl=sl, o_ref=o_ref, incl=incl, strict=strict,
                q=q_ref[:, sl], k=k_ref[:, sl], v=v_ref[:, sl],
                beta=col_ref[:, lbeta:lbeta + 1], gcol=col_ref[:, lg:lg + 1], grow=grow,
                glast=grow[:, n - 1:n] if d == 0 else grow[:, 0:1]))
    for c in chains:
        c["decay"] = jnp.exp(jnp.where(c["incl"], c["gcol"] - c["grow"], NEG))
        c["kb"] = c["k"] * c["beta"]
        c["egc"] = jnp.exp(c["gcol"])
    for c in chains:
        c["a"] = jnp.where(c["strict"], _bdot_nt(c["kb"], c["k"]) * c["decay"], 0.0)
        c["attn"] = _bdot_nt(c["q"], c["k"]) * c["decay"]
    eye = jnp.where(rid == cid, 1.0, 0.0)
    for c in chains:
        c["t"] = eye - jnp.where((rid >> 1) == (cid >> 1), c["a"], 0.0)
    sb = 1
    while (1 << sb) < n:
        off = ((rid >> (sb + 1)) == (cid >> (sb + 1))) & ((rid >> sb) != (cid >> sb))
        for c in chains:
            c["x"] = _bdot(jnp.where(off, c["a"], 0.0), c["t"])
        for c in chains:
            c["t"] = c["t"] - _bdot(c["t"], c["x"])
        sb += 1
    for c in chains:
        c["w"] = _bdot(c["t"], c["kb"] * c["egc"])
        c["u"] = _bdot(c["t"], c["v"] * c["beta"])
        c["s"] = s_scr[c["ch"]]
    for c in chains:
        c["v_new"] = c["u"] - _bdot(c["w"], c["s"])
        c["o"] = _bdot(c["q"] * c["egc"], c["s"])
    for c in chains:
        c["o_ref"][:, c["sl"]] = c["o"] + _bdot(c["attn"], c["v_new"])
        kd = c["k"] * jnp.exp(c["glast"] - c["gcol"])
        s_scr[c["ch"]] = jnp.exp(c["glast"]) * c["s"] + _bdot_tn(kd, c["v_new"])


def _gdn_scan(qkv, gcol, grow, n_lat, n_ctx, mh, gh, dk):
    rows = qkv.shape[0]
    gdw = gh * dk
    nc = n_lat + n_ctx
    fc = functools.partial(_fwd_chunk, n_lat=n_lat, n_ctx=n_ctx)
    bc = functools.partial(_bwd_chunk, n_lat=n_lat, n_ctx=n_ctx)

    def specs(cm):
        return [pl.BlockSpec((CHUNK, gdw), lambda j, p=p: (cm(j), p)) for p in range(3)] + [
            pl.BlockSpec((CHUNK, LANES), lambda j: (cm(j), 0)),
            pl.BlockSpec((1, LANES, CHUNK), lambda j: (cm(j), 0, 0))]

    blk = 2 * (3 * CHUNK * gdw * 4 + 2 * CHUNK * LANES * 4 + CHUNK * gdw * 4)
    return pl.pallas_call(
        functools.partial(_gdn_body, mh=mh, gh=gh, dk=dk),
        out_shape=(jax.ShapeDtypeStruct((rows, gdw), f32), jax.ShapeDtypeStruct((rows, gdw), f32)),
        grid=(nc,),
        in_specs=specs(fc) + specs(bc),
        out_specs=(pl.BlockSpec((CHUNK, gdw), lambda j: (fc(j), 0)),
                   pl.BlockSpec((CHUNK, gdw), lambda j: (bc(j), 0))),
        scratch_shapes=[pltpu.VMEM((2 * gh, dk, dk), f32)],
        compiler_params=_params(blk, 2 * gh * dk * dk * 4, semantics=("arbitrary",)),
        name="gdn_scan",
    )(qkv, qkv, qkv, gcol, grow, qkv, qkv, qkv, gcol, grow)


def _head_rms(t, g):
    return t * lax.rsqrt(jnp.mean(t * t, axis=-1, keepdims=True) + EPS) * g


def _mix_ml_body(hf_ref, hb_ref, o_gate_ref, g_ref, o_ref, *, mh, dh):
    for h in range(mh):
        sl = slice(h * dh, (h + 1) * dh)
        y = _head_rms(hf_ref[:, sl] + hb_ref[:, sl], g_ref[:, sl])
        o_ref[:, sl] = (_sigmoid(o_gate_ref[:, sl]) * y).astype(bf16)


def _mix_ml(hf, hb, pm, g_row, t, mh, dh, tm):
    mlw = mh * dh
    return pl.pallas_call(
        functools.partial(_mix_ml_body, mh=mh, dh=dh),
        out_shape=jax.ShapeDtypeStruct((t, mlw), bf16),
        grid=(t // tm,),
        in_specs=[pl.BlockSpec((tm, mlw), lambda i: (i, 0)),
                  pl.BlockSpec((tm, mlw), lambda i: (i, 0)),
                  pl.BlockSpec((tm, mlw), lambda i: (i, 3)),
                  pl.BlockSpec((1, mlw), lambda i: (0, 0))],
        out_specs=pl.BlockSpec((tm, mlw), lambda i: (i, 0)),
        compiler_params=_params(4 * tm * mlw * 4, semantics=("arbitrary",)),
        name="mix_ml",
    )(hf, hb, pm, g_row)


def _mix_gd_body(hf_ref, hb_ref, z_ref, g_ref, o_ref, *, gh, dk, ncol):
    for h in range(gh):
        sl = slice(h * dk, (h + 1) * dk)
        y = _head_rms(hf_ref[:, :, sl] + hb_ref[:, :, sl], g_ref[:, sl])
        o_ref[:, :, sl] = jnp.swapaxes(_silu(z_ref[:, :, sl]) * y, 0, 1).astype(bf16)


def _mix_gd(hf3, hb3, pg3, g_row, r, w, gh, dk, rb):
    gdw = gh * dk
    ncol = 16
    return pl.pallas_call(
        functools.partial(_mix_gd_body, gh=gh, dk=dk, ncol=ncol),
        out_shape=jax.ShapeDtypeStruct((r, w, gdw), bf16),
        grid=(w // ncol, r // rb),
        in_specs=[pl.BlockSpec((ncol, rb, gdw), lambda c, j: (c, j, 0)),
                  pl.BlockSpec((ncol, rb, gdw), lambda c, j: (c, j, 0)),
                  pl.BlockSpec((ncol, rb, gdw), lambda c, j: (c, j, 3)),
                  pl.BlockSpec((1, gdw), lambda c, j: (0, 0))],
        out_specs=pl.BlockSpec((rb, ncol, gdw), lambda c, j: (j, c, 0)),
        compiler_params=_params(4 * ncol * rb * gdw * 4, semantics=("arbitrary", "arbitrary")),
        name="mix_gd",
    )(hf3, hb3, pg3, g_row)


def _outproj_body(a1_ref, a2_ref, b1_ref, b2_ref, x_ref, gate_ref, o_ref):
    y = _bdot(a1_ref[...], b1_ref[...]) + _bdot(a2_ref[...], b2_ref[...])
    o_ref[...] = x_ref[...] + gate_ref[0:1, :] * y


def _outproj(a1, a2, w1, w2, x, mod, gate_blk, tm, tn):
    t, k1 = a1.shape
    k2 = a2.shape[1]
    d = x.shape[1]
    tn = min(tn, d)
    gb = gate_blk * (d // tn)
    return pl.pallas_call(
        _outproj_body,
        out_shape=jax.ShapeDtypeStruct((t, d), f32),
        grid=(t // tm, d // tn),
        in_specs=[pl.BlockSpec((tm, k1), lambda i, j: (i, 0)),
                  pl.BlockSpec((tm, k2), lambda i, j: (i, 0)),
                  pl.BlockSpec((k1, tn), lambda i, j: (0, j)),
                  pl.BlockSpec((k2, tn), lambda i, j: (0, j)),
                  pl.BlockSpec((tm, tn), lambda i, j: (i, j)),
                  pl.BlockSpec((8, tn), lambda i, j: (0, gb + j))],
        out_specs=pl.BlockSpec((tm, tn), lambda i, j: (i, j)),
        compiler_params=_params(tm * (k1 + k2) * 2 + (k1 + k2) * tn * 2 + 2 * tm * tn * 4,
                                semantics=("arbitrary", "arbitrary")),
        name="outproj",
    )(a1, a2, w1, w2, x, mod)


def _topk_rows(s, kk, n_rows):
    rid = lax.broadcasted_iota(i32, s.shape, 0)
    vals, idxs = [], []
    for _ in range(kk):
        m = jnp.max(s, axis=0, keepdims=True)
        idx = jnp.min(jnp.where(s == m, rid, n_rows), axis=0, keepdims=True)
        s = jnp.where(rid == idx, -jnp.inf, s)
        vals.append(m)
        idxs.append(idx)
    return vals, idxs


def _peer_topk_body(q_ref, keys_ref, e_ref, g_ref, *, ph, nk, half):
    kk = PEER_TOPK
    tb = q_ref.shape[0]
    e_rows, g_rows = [], []
    for h in range(ph):
        sv, si = [], []
        for p in range(2):
            grp = 2 * h + p
            qg = q_ref[:, grp * half:(grp + 1) * half]
            kg = keys_ref[grp]
            k_hi, k_mid, _ = _split3(kg)
            q_hi, q_mid, _ = _split3(qg)
            nt = lambda a, b: lax.dot_general(a, b, (((1,), (1,)), ((), ())), preferred_element_type=f32)
            s = nt(k_hi, q_hi) + (nt(k_hi, q_mid) + nt(k_mid, q_hi))
            vals, idxs = _topk_rows(s, kk, nk)
            sv.append(vals)
            si.append(idxs)
        pairs = [(a, b) for a in range(kk) for b in range(kk) if (a + 1) * (b + 1) <= kk]
        npad = -len(pairs) % 8
        cand = jnp.concatenate([sv[0][a] + sv[1][b] for a, b in pairs]
                               + [jnp.full((npad, tb), -jnp.inf, f32)], axis=0)
        eidx = jnp.concatenate([si[0][a] * nk + si[1][b] for a, b in pairs]
                               + [jnp.full((npad, tb), -1, i32)], axis=0)
        n_cand = len(pairs) + npad
        rid = lax.broadcasted_iota(i32, cand.shape, 0)
        fv, fe = [], []
        for _ in range(kk):
            m = jnp.max(cand, axis=0, keepdims=True)
            pos = jnp.min(jnp.where(cand == m, rid, n_cand), axis=0, keepdims=True)
            hit = rid == pos
            fe.append(jnp.max(jnp.where(hit, eidx, -1), axis=0, keepdims=True))
            cand = jnp.where(hit, -jnp.inf, cand)
            fv.append(m)
        fv = jnp.concatenate(fv, axis=0)
        ex = jnp.exp(fv - fv[0:1, :])
        g_rows.append(ex / jnp.sum(ex, axis=0, keepdims=True))
        e_rows.append(jnp.concatenate(fe, axis=0))
    e_ref[...] = jnp.concatenate(e_rows, axis=0).T
    g_ref[...] = jnp.concatenate(g_rows, axis=0).T


def _peer_topk(qp, keys, ph, nk, half, tb):
    t = qp.shape[0]
    hk = ph * PEER_TOPK
    return pl.pallas_call(
        functools.partial(_peer_topk_body, ph=ph, nk=nk, half=half),
        out_shape=(jax.ShapeDtypeStruct((t, hk), i32), jax.ShapeDtypeStruct((t, hk), f32)),
        grid=(t // tb,),
        in_specs=[pl.BlockSpec((tb, 2 * ph * half), lambda i: (i, 0)),
                  pl.BlockSpec((2 * ph, nk, half), lambda i: (0, 0, 0))],
        out_specs=(pl.BlockSpec((tb, hk), lambda i: (i, 0)), pl.BlockSpec((tb, hk), lambda i: (i, 0))),
        compiler_params=_params(tb * 2 * ph * half * 4 + 2 * ph * nk * half * 4 + 2 * tb * hk * 4,
                                semantics=("arbitrary",)),
        name="peer_topk",
    )(qp, keys)


def _peer_w_body(e_ref, g_ref, w_ref, *, nk):
    tb, hk = e_ref.shape
    shift = nk.bit_length() - 1
    sub = lax.broadcasted_iota(i32, (nk, hk), 0)

    def group(t8, carry):
        base = pl.multiple_of(t8 * 8, 8)
        e8 = e_ref[pl.ds(base, 8), :]
        g8 = g_ref[pl.ds(base, 8), :]
        ws = []
        for t in range(8):
            e = e8[t:t + 1, :]
            g = g8[t:t + 1, :]
            ei = lax.shift_right_logical(e, shift)
            ej = e & (nk - 1)
            g_hi = g.astype(bf16).astype(f32)
            pt = jnp.where(sub == ei, 1.0, 0.0).astype(bf16)
            hit = sub == ej
            qt = jnp.concatenate([jnp.where(hit, g_hi, 0.0).astype(bf16),
                                  jnp.where(hit, g - g_hi, 0.0).astype(bf16)], axis=1)
            ws.append(lax.dot_general(jnp.concatenate([pt, pt], axis=1), qt, (((1,), (1,)), ((), ())),
                                      preferred_element_type=f32))
        w_ref[:, pl.ds(base, 8), :] = jnp.swapaxes(jnp.stack(ws, axis=0), 0, 1)
        return carry

    lax.fori_loop(0, tb // 8, group, 0)


def _peer_weights(e, g, nk, tb):
    t, hk = e.shape
    return pl.pallas_call(
        functools.partial(_peer_w_body, nk=nk),
        out_shape=jax.ShapeDtypeStruct((nk, t, nk), f32),
        grid=(t // tb,),
        in_specs=[pl.BlockSpec((tb, hk), lambda i: (i, 0)), pl.BlockSpec((tb, hk), lambda i: (i, 0))],
        out_specs=pl.BlockSpec((nk, tb, nk), lambda i: (0, i, 0)),
        compiler_params=_params(tb * nk * nk * 4 + 2 * tb * hk * 4, semantics=("arbitrary",)),
        name="peer_weights",
    )(e, g)


def _gelu(x):
    return 0.5 * x * (1.0 + lax.erf(x * 0.7071067811865476))


def _peer_act_body(f_ref, u_ref, w_ref, o_ref, *, nk, gi):
    acc = _bdot_nt(f_ref[...], u_ref[...])
    for gg in range(gi):
        sl = slice(gg * nk, (gg + 1) * nk)
        o_ref[:, sl] = (w_ref[gg] * _gelu(acc[:, sl])).astype(bf16)


def _peer_act(f, u, w3, nk, tm):
    t, d = f.shape
    ne = u.shape[0]
    gi = 4
    tn = gi * nk
    return pl.pallas_call(
        functools.partial(_peer_act_body, nk=nk, gi=gi),
        out_shape=jax.ShapeDtypeStruct((t, ne), bf16),
        grid=(t // tm, ne // tn),
        in_specs=[pl.BlockSpec((tm, d), lambda i, j: (i, 0)),
                  pl.BlockSpec((tn, d), lambda i, j: (j, 0)),
                  pl.BlockSpec((gi, tm, nk), lambda i, j: (j, i, 0))],
        out_specs=pl.BlockSpec((tm, tn), lambda i, j: (i, j)),
        compiler_params=_params(tm * d * 2 + tn * d * 2 + tm * tn * 4 + tm * tn * 2 + tm * tn * 4,
                                semantics=("arbitrary", "arbitrary")),
        name="peer_act",
    )(f, u, w3)


def _peer_out_body(a_ref, v_ref, o_ref):
    @pl.when(pl.program_id(2) == 0)
    def _():
        o_ref[...] = jnp.zeros_like(o_ref)

    o_ref[...] += jnp.dot(a_ref[...], v_ref[...], preferred_element_type=f32)


def _peer_out(a, v, tm, tn, tk):
    t, ne = a.shape
    d = v.shape[1]
    tn = min(tn, d)
    tk = min(tk, ne)
    return pl.pallas_call(
        _peer_out_body,
        out_shape=jax.ShapeDtypeStruct((t, d), f32),
        grid=(t // tm, d // tn, ne // tk),
        in_specs=[pl.BlockSpec((tm, tk), lambda i, j, k: (i, k)),
                  pl.BlockSpec((tk, tn), lambda i, j, k: (k, j))],
        out_specs=pl.BlockSpec((tm, tn), lambda i, j, k: (i, j)),
        compiler_params=_params(tm * tk * 2 + tk * tn * 2 + 2 * tm * tn * 4,
                                semantics=("arbitrary", "arbitrary", "arbitrary")),
        name="peer_out",
    )(a, v)


def _final_body(x_ref, p_ref, gate_ref, g_ref, o_ref):
    x = x_ref[...] + gate_ref[0:1, :] * p_ref[...]
    o_ref[...] = x * lax.rsqrt(jnp.mean(x * x, axis=-1, keepdims=True) + EPS) * g_ref[...]


def _final(x, p, mod, gate_blk, g, tm):
    t, d = x.shape
    return pl.pallas_call(
        _final_body,
        out_shape=jax.ShapeDtypeStruct((t, d), f32),
        grid=(t // tm,),
        in_specs=[pl.BlockSpec((tm, d), lambda i: (i, 0)), pl.BlockSpec((tm, d), lambda i: (i, 0)),
                  pl.BlockSpec((8, d), lambda i: (0, gate_blk)), pl.BlockSpec((1, d), lambda i: (0, 0))],
        out_specs=pl.BlockSpec((tm, d), lambda i: (i, 0)),
        compiler_params=_params(3 * tm * d * 4, semantics=("arbitrary",)),
        name="final_norm",
    )(x, p, mod, g)


def _pick(n, pref):
    t = min(pref, n)
    while n % t:
        t -= 8
    return t


def kernel(x, c, ctx, c_ctx, w_mod, b_mod, g_mix, w_in, b_gate, gd_conv_w, gd_conv_b, gd_a_log,
           ml_norm_g, gd_norm_g, w_out, g_ffn, peer_wq, peer_keys, peer_u, peer_v, g_final):
    return _block(x, c, ctx, c_ctx, w_mod, b_mod, g_mix, w_in, b_gate, gd_conv_w, gd_conv_b, gd_a_log,
                  ml_norm_g, gd_norm_g, w_out, g_ffn, peer_wq, peer_keys, peer_u, peer_v, g_final)[0]


def _block(x, c, ctx, c_ctx, w_mod, b_mod, g_mix, w_in, b_gate, gd_conv_w, gd_conv_b, gd_a_log,
           ml_norm_g, gd_norm_g, w_out, g_ffn, peer_wq, peer_keys, peer_u, peer_v, g_final):
    assert w_mod.shape[0] == 1 and x.shape[0] == 1, "one layer, one batch element"
    _, t, d = x.shape
    tc = ctx.shape[1]
    _, mh, dh = ml_norm_g.shape
    _, gh, dk = gd_norm_g.shape
    mlw, gdw = mh * dh, gh * dk
    _, ph, _, nk, half = peer_keys.shape
    ne = peer_u.shape[1]
    w = GRID_W
    r = t // w
    assert r == tc and t % CHUNK == 0 and tc % CHUNK == 0 and r % CHUNK == 0
    assert ne == nk * nk and nk & (nk - 1) == 0 and 4 * mh + 4 * gh <= LANES
    n_lat, n_ctx = t // CHUNK, tc // CHUNK

    x2 = x.reshape(t, d)
    ctx2 = ctx.reshape(tc, d)

    s8 = jnp.zeros((8, d), f32).at[0].set(c[0]).at[1].set(c_ctx)
    mod = _mod_vectors(s8, w_mod[0], b_mod)

    h_rm = _normmod(x2, ctx2, g_mix, mod, 0, 1, tm=tc)
    h_cm = _normmod_colmajor(x2.reshape(r, w, d), g_mix[0:1], mod, 0, 1, rb=_pick(r, 64)).reshape(t, d)

    main = 4 * mlw + 4 * gdw
    w_in0 = w_in[0]
    w_ml = w_in0[:, :4 * mlw].astype(bf16)
    w_gd = w_in0[:, 4 * mlw:main].astype(bf16)
    ng = w_in0.shape[1] - main
    w_gate = jnp.pad(w_in0[:, main:], ((0, 0), (0, LANES - ng))).astype(bf16)
    b_gate_row = jnp.pad(b_gate, ((0, 0), (0, LANES - ng)))
    rows = t + tc
    tm_all = _pick(rows, 1280)
    tm_lat = _pick(t, 1024)
    pm = _matmul(h_rm, w_ml, m=rows, tm=tm_all, tn=512, name="proj_ml")
    pg_lat = _matmul(h_cm, w_gd, m=t, tm=tm_lat, tn=512, name="proj_gd_lat")
    pg_ctx = _matmul(h_rm, w_gd, m=tc, tm=tc, tn=512, a_off=t // tc, name="proj_gd_ctx")
    bias_spec = pl.BlockSpec((1, LANES), lambda i, j: (0, 0))
    add_bias = lambda acc, b_ref: acc + b_ref[...]
    g_rm = _matmul(h_rm, w_gate, m=rows, tm=tm_all, tn=LANES, extras=[(b_gate_row, bias_spec)],
                   epilogue=add_bias, name="proj_gate_rm")
    g_cm = _matmul(h_cm, w_gate, m=t, tm=tm_lat, tn=LANES, extras=[(b_gate_row, bias_spec)],
                   epilogue=add_bias, name="proj_gate_cm")

    o_a = 4 * mh + 2 * gh
    alog_row = jnp.zeros((1, LANES), f32).at[0, o_a:o_a + 2 * gh].set(gd_a_log.reshape(-1))
    ml_col, ml_row = _gate_prep(g_rm, n_lat, g_rm, n_lat, n_ctx, alog_row, mh, gh)
    gd_col, gd_row = _gate_prep(g_cm, n_lat, g_rm, n_lat, n_ctx, alog_row, mh, gh)

    hm_f, hm_b = _mlstm_scan(pm, ml_col, ml_row, n_lat, n_ctx, mh, dh)

    qkv = _gdn_conv(pg_lat, pg_ctx, gd_conv_w[0], gd_conv_b, gdw, dk)
    hg_f, hg_b = _gdn_scan(qkv, gd_col, gd_row, n_lat, n_ctx, mh, gh, dk)

    mix_ml = _mix_ml(hm_f, hm_b, pm, ml_norm_g.reshape(1, mlw), t, mh, dh, tm=_pick(t, 256))
    ncols = rows // r
    mix_gd = _mix_gd(hg_f.reshape(ncols, r, gdw), hg_b.reshape(ncols, r, gdw),
                     pg_lat.reshape(w, r, 4 * gdw), gd_norm_g.reshape(1, gdw), r, w, gh, dk,
                     rb=_pick(r, 32)).reshape(t, gdw)
    w_out0 = w_out[0].astype(bf16)
    x1 = _outproj(mix_ml, mix_gd, w_out0[:mlw], w_out0[mlw:], x2, mod, 2, tm=tm_lat, tn=512)

    f_in = _normmod(x1, None, g_ffn, mod, 3, 4, tm=_pick(t, 256))
    qp = _matmul(f_in, peer_wq[0].astype(bf16), m=t, tm=tm_lat, tn=512, name="peer_query")
    e_idx, gates = _peer_topk(qp, peer_keys[0].reshape(2 * ph, nk, half), ph, nk, half, tb=_pick(t, 256))
    w3 = _peer_weights(e_idx, gates, nk, tb=_pick(t, 128))
    act = _peer_act(f_in, peer_u[0].astype(bf16), w3, nk, tm=_pick(t, 1024))
    x2_out = _peer_out(act, peer_v[0].astype(bf16), tm=_pick(t, 1024), tn=2048, tk=1024)

    out = _final(x1, x2_out, mod, 5, g_final.reshape(1, d), tm=_pick(t, 256))
    aux = dict(mod=mod, h_rm=h_rm, h_cm=h_cm, pm=pm, pg_lat=pg_lat, pg_ctx=pg_ctx, g_rm=g_rm, g_cm=g_cm,
               ml_col=ml_col, ml_row=ml_row, gd_col=gd_col, gd_row=gd_row, hm_f=hm_f, hm_b=hm_b, qkv=qkv,
               hg_f=hg_f, hg_b=hg_b, mix_ml=mix_ml, mix_gd=mix_gd, x1=x1, f_in=f_in, qp=qp, e_idx=e_idx,
               gates=gates, w3=w3, act=act, x2_out=x2_out)
    return out.reshape(1, t, d), aux
```

```python
import functools
import math

import jax
import jax.numpy as jnp
from jax import lax
from jax.experimental import pallas as pl
from jax.experimental.pallas import tpu as pltpu

f32 = jnp.float32
bf16 = jnp.bfloat16
i32 = jnp.int32

GRID_W = 64
PEER_TOPK = 16
EPS = 1e-6
NEG = -1e30
CHUNK = 128
LANES = 128
VMEM_CAP = 56 * 1024 * 1024


def _params(block_bytes, scratch_bytes=0, semantics=None):
    est = 2 * block_bytes + scratch_bytes + 16 * 1024 * 1024
    return pltpu.CompilerParams(
        dimension_semantics=semantics,
        vmem_limit_bytes=int(min(max(est, 32 * 1024 * 1024), VMEM_CAP)))


def _bdot(a, b):
    return jnp.dot(a.astype(bf16), b.astype(bf16), preferred_element_type=f32)


def _bdot_nt(a, b):
    return lax.dot_general(a.astype(bf16), b.astype(bf16), (((1,), (1,)), ((), ())),
                           preferred_element_type=f32)


def _bdot_tn(a, b):
    return lax.dot_general(a.astype(bf16), b.astype(bf16), (((0,), (0,)), ((), ())),
                           preferred_element_type=f32)


def _split3(x):
    hi = x.astype(bf16)
    r = x - hi.astype(f32)
    mid = r.astype(bf16)
    lo = (r - mid.astype(f32)).astype(bf16)
    return hi, mid, lo


def _softplus(x):
    return jnp.maximum(x, 0.0) + jnp.log1p(jnp.exp(-jnp.abs(x)))


def _sigmoid(x):
    return 1.0 / (1.0 + jnp.exp(-x))


def _silu(x):
    return x * _sigmoid(x)


def _mod_body(s_ref, w_ref, b_ref, o_ref):
    o_ref[...] = _bdot(_silu(s_ref[...]), w_ref[...]) + b_ref[...]


def _mod_vectors(s8, w_mod, b_mod):
    d, n = w_mod.shape
    tn = min(512, n)
    return pl.pallas_call(
        _mod_body,
        out_shape=jax.ShapeDtypeStruct((8, n), f32),
        grid=(n // tn,),
        in_specs=[pl.BlockSpec((8, d), lambda j: (0, 0)),
                  pl.BlockSpec((d, tn), lambda j: (0, j)),
                  pl.BlockSpec((1, tn), lambda j: (0, j))],
        out_specs=pl.BlockSpec((8, tn), lambda j: (0, j)),
        compiler_params=_params(d * tn * 4 + 8 * d * 4, semantics=("arbitrary",)),
        name="mod_vectors",
    )(s8, w_mod, b_mod)


def _norm_rows(x, g, shift, scale):
    y = x * lax.rsqrt(jnp.mean(x * x, axis=-1, keepdims=True) + EPS)
    return (y * g) * (1.0 + scale) + shift


def _normmod_body(x_ref, c_ref, g_ref, sh_ref, sc_ref, o_ref, *, n_lat):
    is_ctx = pl.program_id(0) >= n_lat
    x = jnp.where(is_ctx, c_ref[...], x_ref[...])
    sh = jnp.where(is_ctx, sh_ref[1:2, :], sh_ref[0:1, :])
    sc = jnp.where(is_ctx, sc_ref[1:2, :], sc_ref[0:1, :])
    o_ref[...] = _norm_rows(x, g_ref[...], sh, sc).astype(bf16)


def _normmod(x, ctx, g, mod, shift_blk, scale_blk, tm):
    t, d = x.shape
    n_lat = t // tm
    if ctx is None:
        ctx = x
        n_ctx = 0
    else:
        n_ctx = ctx.shape[0] // tm
    rows = (n_lat + n_ctx) * tm
    return pl.pallas_call(
        functools.partial(_normmod_body, n_lat=n_lat),
        out_shape=jax.ShapeDtypeStruct((rows, d), bf16),
        grid=(n_lat + n_ctx,),
        in_specs=[pl.BlockSpec((tm, d), lambda i: (jnp.minimum(i, n_lat - 1), 0)),
                  pl.BlockSpec((tm, d), lambda i: (jnp.maximum(i - n_lat, 0), 0)),
                  pl.BlockSpec((1, d), lambda i: (0, 0)),
                  pl.BlockSpec((8, d), lambda i: (0, shift_blk)),
                  pl.BlockSpec((8, d), lambda i: (0, scale_blk))],
        out_specs=pl.BlockSpec((tm, d), lambda i: (i, 0)),
        compiler_params=_params(3 * tm * d * 4, semantics=("arbitrary",)),
        name="normmod",
    )(x, ctx, g, mod, mod)


def _normmod_cm_body(x_ref, g_ref, sh_ref, sc_ref, o_ref, *, ncol):
    x = jnp.swapaxes(x_ref[...], 0, 1)
    o_ref[...] = _norm_rows(x, g_ref[...], sh_ref[0:1, :], sc_ref[0:1, :]).astype(bf16)


def _normmod_colmajor(x3, g, mod, shift_blk, scale_blk, rb):
    r, w, d = x3.shape
    ncol = 8
    return pl.pallas_call(
        functools.partial(_normmod_cm_body, ncol=ncol),
        out_shape=jax.ShapeDtypeStruct((w, r, d), bf16),
        grid=(w // ncol, r // rb),
        in_specs=[pl.BlockSpec((rb, ncol, d), lambda c, j: (j, c, 0)),
                  pl.BlockSpec((1, d), lambda c, j: (0, 0)),
                  pl.BlockSpec((8, d), lambda c, j: (0, shift_blk)),
                  pl.BlockSpec((8, d), lambda c, j: (0, scale_blk))],
        out_specs=pl.BlockSpec((ncol, rb, d), lambda c, j: (c, j, 0)),
        compiler_params=_params(2 * rb * ncol * d * 4, semantics=("arbitrary", "arbitrary")),
        name="normmod_colmajor",
    )(x3, g, mod, mod)


def _mm_body(a_ref, b_ref, *rest, nt, epilogue):
    acc = (_bdot_nt if nt else _bdot)(a_ref[...], b_ref[...])
    o_ref = rest[-1]
    o_ref[...] = epilogue(acc, *rest[:-1]).astype(o_ref.dtype)


def _matmul(a, b, *, m, tm, tn, nt=False, a_off=0, n=None, b_off=0, extras=(), epilogue=None,
            out_dtype=f32, name="matmul"):
    k = a.shape[1]
    if n is None:
        n = b.shape[0] if nt else b.shape[1]
    tn = min(tn, n)
    if epilogue is None:
        epilogue = lambda acc: acc
    b_spec = (pl.BlockSpec((tn, k), lambda i, j: (j, 0)) if nt
              else pl.BlockSpec((k, tn), lambda i, j: (0, j + b_off)))
    extra_bytes = sum(jnp.dtype(e.dtype).itemsize * math.prod(s.block_shape) for e, s in extras)
    return pl.pallas_call(
        functools.partial(_mm_body, nt=nt, epilogue=epilogue),
        out_shape=jax.ShapeDtypeStruct((m, n), out_dtype),
        grid=(m // tm, n // tn),
        in_specs=[pl.BlockSpec((tm, k), lambda i, j: (i + a_off, 0)), b_spec] + [s for _, s in extras],
        out_specs=pl.BlockSpec((tm, tn), lambda i, j: (i, j)),
        compiler_params=_params(tm * k * 2 + k * tn * 2 + tm * tn * 4 + extra_bytes,
                                semantics=("arbitrary", "arbitrary")),
        name=name,
    )(a, b, *[e for e, _ in extras])


def _gate_body(ga_ref, gb_ref, alog_ref, col_ref, row_ref, *, n_a, mh, gh):
    o_f, o_b, o_a = 2 * mh, 4 * mh, 4 * mh + 2 * gh
    is_b = pl.program_id(0) >= n_a
    g = jnp.where(is_b, gb_ref[...], ga_ref[...])
    n = g.shape[0]
    lane = lax.broadcasted_iota(i32, (n, LANES), 1)
    lf = -_softplus(-g)
    gl = -jnp.exp(alog_ref[...]) * _softplus(g)
    src = jnp.where(lane < o_b, lf, gl)
    rid = lax.broadcasted_iota(i32, (n, n), 0)
    cid = lax.broadcasted_iota(i32, (n, n), 1)
    tri_f = jnp.where(cid <= rid, 1.0, 0.0).astype(bf16)
    tri_b = jnp.where(cid >= rid, 1.0, 0.0).astype(bf16)
    parts = _split3(src)
    cf = sum(jnp.dot(tri_f, p, preferred_element_type=f32) for p in parts)
    cb = sum(jnp.dot(tri_b, p, preferred_element_type=f32) for p in parts)
    fwd = ((lane >= o_f) & (lane < o_f + mh)) | ((lane >= o_a) & (lane < o_a + gh))
    bwd = ((lane >= o_f + mh) & (lane < o_b)) | ((lane >= o_a + gh) & (lane < o_a + 2 * gh))
    beta = (lane >= o_b) & (lane < o_a)
    out = jnp.where(lane < o_f, g,
                    jnp.where(fwd, cf, jnp.where(bwd, cb, jnp.where(beta, _sigmoid(g), 0.0))))
    col_ref[...] = out
    row_ref[0] = out.T


def _gate_prep(g_a, n_a, g_b, b_off, n_b, alog_row, mh, gh):
    nc = n_a + n_b
    return pl.pallas_call(
        functools.partial(_gate_body, n_a=n_a, mh=mh, gh=gh),
        out_shape=(jax.ShapeDtypeStruct((nc * CHUNK, LANES), f32),
                   jax.ShapeDtypeStruct((nc, LANES, CHUNK), f32)),
        grid=(nc,),
        in_specs=[pl.BlockSpec((CHUNK, LANES), lambda i: (jnp.minimum(i, n_a - 1), 0)),
                  pl.BlockSpec((CHUNK, LANES), lambda i: (jnp.maximum(i - n_a, 0) + b_off, 0)),
                  pl.BlockSpec((1, LANES), lambda i: (0, 0))],
        out_specs=(pl.BlockSpec((CHUNK, LANES), lambda i: (i, 0)),
                   pl.BlockSpec((1, LANES, CHUNK), lambda i: (i, 0, 0))),
        compiler_params=_params(4 * CHUNK * LANES * 4, semantics=("arbitrary",)),
        name="gate_prep",
    )(g_a, g_b, alog_row)


def _fwd_chunk(j, n_lat, n_ctx):
    return jnp.where(j < n_ctx, n_lat + j, j - n_ctx)


def _bwd_chunk(j, n_lat, n_ctx):
    return n_lat + n_ctx - 1 - j


def _tri_masks(n):
    rid = lax.broadcasted_iota(i32, (n, n), 0)
    cid = lax.broadcasted_iota(i32, (n, n), 1)
    return cid <= rid, cid >= rid, cid < rid, cid > rid


def _mlstm_body(qf_ref, kf_ref, vf_ref, cf_ref, rf_ref, qb_ref, kb_ref, vb_ref, cb_ref, rb_ref,
                hf_ref, hb_ref, c_scr, n_scr, m_scr, *, mh, dh):
    @pl.when(pl.program_id(0) == 0)
    def _():
        c_scr[...] = jnp.zeros_like(c_scr)
        n_scr[...] = jnp.zeros_like(n_scr)
        m_scr[...] = jnp.full_like(m_scr, NEG)

    n = CHUNK
    incl_f, incl_b, _, _ = _tri_masks(n)
    scale = dh ** -0.5
    dirs = ((0, qf_ref, kf_ref, vf_ref, cf_ref, rf_ref, hf_ref, incl_f),
            (1, qb_ref, kb_ref, vb_ref, cb_ref, rb_ref, hb_ref, incl_b))
    chains = []
    for d, q_ref, k_ref, v_ref, col_ref, row_ref, h_ref, incl in dirs:
        for h in range(mh):
            ch = d * mh + h
            li, lb = d * mh + h, 2 * mh + d * mh + h
            sl = slice(h * dh, (h + 1) * dh)
            brow = row_ref[0, lb:lb + 1, :]
            chains.append(dict(
                ch=ch, sl=sl, h_ref=h_ref, incl=incl,
                q=q_ref[:, sl] * scale, k=k_ref[:, sl], v=v_ref[:, sl],
                icol=col_ref[:, li:li + 1], bcol=col_ref[:, lb:lb + 1],
                irow=row_ref[0, li:li + 1, :], brow=brow,
                btot=brow[:, n - 1:n] if d == 0 else brow[:, 0:1],
                m_st=m_scr[ch, :, 0:1], c_st=c_scr[ch], n_st=n_scr[ch]))
    reps = dh // LANES
    ones = jnp.ones((n, LANES), bf16)
    wide = lambda x: jnp.concatenate([x] * reps, axis=1)
    for c in chains:
        bcol_bc = jnp.broadcast_to(c["bcol"], (n, LANES))
        dmat = jnp.where(c["incl"], bcol_bc - c["brow"] + c["irow"], NEG)
        m_row = jnp.maximum(c["bcol"] + c["m_st"], jnp.max(dmat, axis=1, keepdims=True))
        c["m_row"] = jnp.broadcast_to(m_row, (n, LANES))
        c["p"] = jnp.exp(dmat - c["m_row"])
        c["w_inter"] = jnp.exp(bcol_bc + c["m_st"] - c["m_row"])
    for c in chains:
        c["s"] = _bdot_nt(c["q"], c["k"]) * c["p"]
        c["qc"] = _bdot(c["q"], c["c_st"])
        c["qn"] = _bdot(c["q"], c["n_st"])
    for c in chains:
        s_hi = c["s"].astype(bf16)
        s_lo = (c["s"] - s_hi.astype(f32)).astype(bf16)
        row_sum = (jnp.dot(s_hi, ones, preferred_element_type=f32)
                   + jnp.dot(s_lo, ones, preferred_element_type=f32))
        den = c["w_inter"] * c["qn"] + row_sum
        inv = 1.0 / jnp.maximum(jnp.abs(den), jnp.exp(-c["m_row"]))
        num = wide(c["w_inter"]) * c["qc"] + _bdot(c["s"], c["v"])
        c["h_ref"][:, c["sl"]] = num * wide(inv)
    eye = jnp.where(lax.broadcasted_iota(i32, (dh, dh), 0) == lax.broadcasted_iota(i32, (dh, dh), 1),
                    1.0, 0.0).astype(bf16)
    for c in chains:
        ch = c["ch"]
        wlog = c["btot"] - c["bcol"] + c["icol"]
        m_new = jnp.maximum(c["btot"] + c["m_st"], jnp.max(wlog, axis=0, keepdims=True))
        a_st = jnp.exp(c["btot"] + c["m_st"] - m_new)
        kw = c["k"] * wide(jnp.broadcast_to(jnp.exp(wlog - m_new), (n, LANES)))
        kw_t = _bdot_nt(eye, kw)
        c_scr[ch] = a_st * c["c_st"] + _bdot(kw_t, c["v"])
        n_scr[ch] = a_st * c["n_st"] + _bdot(kw_t, ones)
        m_scr[ch] = jnp.broadcast_to(m_new, (1, LANES))


def _mlstm_scan(pm, gcol, grow, n_lat, n_ctx, mh, dh):
    rows = pm.shape[0]
    mlw = mh * dh
    nc = n_lat + n_ctx
    fc = functools.partial(_fwd_chunk, n_lat=n_lat, n_ctx=n_ctx)
    bc = functools.partial(_bwd_chunk, n_lat=n_lat, n_ctx=n_ctx)

    def specs(cm):
        return [pl.BlockSpec((CHUNK, mlw), lambda j, p=p: (cm(j), p)) for p in range(3)] + [
            pl.BlockSpec((CHUNK, LANES), lambda j: (cm(j), 0)),
            pl.BlockSpec((1, LANES, CHUNK), lambda j: (cm(j), 0, 0))]

    blk = 2 * (3 * CHUNK * mlw * 4 + 2 * CHUNK * LANES * 4 + CHUNK * mlw * 4)
    scr = 2 * mh * (dh * dh + dh * LANES + 8 * LANES) * 4
    return pl.pallas_call(
        functools.partial(_mlstm_body, mh=mh, dh=dh),
        out_shape=(jax.ShapeDtypeStruct((rows, mlw), f32), jax.ShapeDtypeStruct((rows, mlw), f32)),
        grid=(nc,),
        in_specs=specs(fc) + specs(bc),
        out_specs=(pl.BlockSpec((CHUNK, mlw), lambda j: (fc(j), 0)),
                   pl.BlockSpec((CHUNK, mlw), lambda j: (bc(j), 0))),
        scratch_shapes=[pltpu.VMEM((2 * mh, dh, dh), f32),
                        pltpu.VMEM((2 * mh, dh, LANES), f32),
                        pltpu.VMEM((2 * mh, 1, LANES), f32)],
        compiler_params=_params(blk, scr, semantics=("arbitrary",)),
        name="mlstm_scan",
    )(pm, pm, pm, gcol, grow, pm, pm, pm, gcol, grow)


def _gdn_conv_body(l_ref, c_ref, p_ref, n_ref, w_ref, b_ref, o_ref, ext, *, n_lat, kw, gdw, dk):
    b = pl.program_id(0)
    cw = o_ref.shape[1]
    rb = o_ref.shape[0]
    is_ctx = b >= n_lat
    half = kw // 2
    ext[0:8, :] = jnp.where((b == 0) | is_ctx, 0.0, p_ref[...])
    ext[8:8 + rb, :] = jnp.where(is_ctx, c_ref[...], l_ref[...])
    ext[8 + rb:16 + rb, :] = jnp.where(b >= n_lat - 1, 0.0, n_ref[...])
    y = jnp.broadcast_to(b_ref[...], (rb, cw))
    for j in range(kw):
        y = y + ext[pl.ds(8 - half + j, rb), :] * w_ref[j:j + 1, :]
    y = _silu(y)
    part = (pl.program_id(1) * cw) // gdw
    qk_scale = jnp.where(part == 0, dk ** -0.5, 1.0)
    for hh in range(cw // dk):
        sl = slice(hh * dk, (hh + 1) * dk)
        t = y[:, sl]
        tn = t * lax.rsqrt(jnp.sum(t * t, axis=-1, keepdims=True) + EPS) * qk_scale
        o_ref[:, sl] = jnp.where(part == 2, t, tn)


def _gdn_conv(pg_lat, pg_ctx, conv_w, conv_b, gdw, dk):
    t = pg_lat.shape[0]
    rb = pg_ctx.shape[0]
    n_lat = t // rb
    kw = conv_w.shape[0]
    cw = gdw
    nh8 = rb // 8
    return pl.pallas_call(
        functools.partial(_gdn_conv_body, n_lat=n_lat, kw=kw, gdw=gdw, dk=dk),
        out_shape=jax.ShapeDtypeStruct((t + rb, 3 * gdw), f32),
        grid=(n_lat + 1, 3 * gdw // cw),
        in_specs=[pl.BlockSpec((rb, cw), lambda b, c: (jnp.minimum(b, n_lat - 1), c)),
                  pl.BlockSpec((rb, cw), lambda b, c: (0, c)),
                  pl.BlockSpec((8, cw), lambda b, c: (jnp.clip(b * nh8 - 1, 0, n_lat * nh8 - 1), c)),
                  pl.BlockSpec((8, cw), lambda b, c: (jnp.minimum((b + 1) * nh8, n_lat * nh8 - 1), c)),
                  pl.BlockSpec((kw, cw), lambda b, c: (0, c)),
                  pl.BlockSpec((1, cw), lambda b, c: (0, c))],
        out_specs=pl.BlockSpec((rb, cw), lambda b, c: (b, c)),
        scratch_shapes=[pltpu.VMEM((rb + 16, cw), f32)],
        compiler_params=_params(4 * rb * cw * 4, (rb + 16) * cw * 4, semantics=("arbitrary", "arbitrary")),
        name="gdn_conv",
    )(pg_lat, pg_ctx, pg_lat, pg_lat, conv_w, conv_b)


def _gdn_body(qf_ref, kf_ref, vf_ref, cf_ref, rf_ref, qb_ref, kb_ref, vb_ref, cb_ref, rb_ref,
              of_ref, ob_ref, s_scr, *, mh, gh, dk):
    @pl.when(pl.program_id(0) == 0)
    def _():
        s_scr[...] = jnp.zeros_like(s_scr)

    n = CHUNK
    incl_f, incl_b, strict_f, strict_b = _tri_masks(n)
    rid = lax.broadcasted_iota(i32, (n, n), 0)
    cid = lax.broadcasted_iota(i32, (n, n), 1)
    o_b, o_a = 4 * mh, 4 * mh + 2 * gh
    dirs = ((0, qf_ref, kf_ref, vf_ref, cf_ref, rf_ref, of_ref, incl_f, strict_f),
            (1, qb_ref, kb_ref, vb_ref, cb_ref, rb_ref, ob_ref, incl_b, strict_b))
    chains = []
    for d, q_ref, k_ref, v_ref, col_ref, row_ref, o_ref, incl, strict in dirs:
        for h in range(gh):
            lbeta, lg = o_b + d * gh + h, o_a + d * gh + h
            sl = slice(h * dk, (h + 1) * dk)
            grow = row_ref[0, lg:lg + 1, :]
            chains.append(dict(
                ch=d * gh + h, sl=sl, o_ref=o_ref, incl=incl, strict=strict,
                q=q_ref[:, sl], k=k_ref[:, sl], v=v_ref[:, sl],
                beta=col_ref[:, lbeta:lbeta + 1], gcol=col_ref[:, lg:lg + 1], grow=grow,
                glast=grow[:, n - 1:n] if d == 0 else grow[:, 0:1]))
    for c in chains:
        c["decay"] = jnp.exp(jnp.where(c["incl"], c["gcol"] - c["grow"], NEG))
        c["kb"] = c["k"] * c["beta"]
        c["egc"] = jnp.exp(c["gcol"])
    for c in chains:
        c["a"] = jnp.where(c["strict"], _bdot_nt(c["kb"], c["k"]) * c["decay"], 0.0)
        c["attn"] = _bdot_nt(c["q"], c["k"]) * c["decay"]
    eye = jnp.where(rid == cid, 1.0, 0.0)
    for c in chains:
        c["t"] = eye - jnp.where((rid >> 1) == (cid >> 1), c["a"], 0.0)
    sb = 1
    while (1 << sb) < n:
        off = ((rid >> (sb + 1)) == (cid >> (sb + 1))) & ((rid >> sb) != (cid >> sb))
        for c in chains:
            c["x"] = _bdot(jnp.where(off, c["a"], 0.0), c["t"])
        for c in chains:
            c["t"] = c["t"] - _bdot(c["t"], c["x"])
        sb += 1
    for c in chains:
        c["w"] = _bdot(c["t"], c["kb"] * c["egc"])
        c["u"] = _bdot(c["t"], c["v"] * c["beta"])
        c["s"] = s_scr[c["ch"]]
    for c in chains:
        c["v_new"] = c["u"] - _bdot(c["w"], c["s"])
        c["o"] = _bdot(c["q"] * c["egc"], c["s"])
    for c in chains:
        c["o_ref"][:, c["sl"]] = c["o"] + _bdot(c["attn"], c["v_new"])
        kd = c["k"] * jnp.exp(c["glast"] - c["gcol"])
        s_scr[c["ch"]] = jnp.exp(c["glast"]) * c["s"] + _bdot_tn(kd, c["v_new"])


def _gdn_scan(qkv, gcol, grow, n_lat, n_ctx, mh, gh, dk):
    rows = qkv.shape[0]
    gdw = gh * dk
    nc = n_lat + n_ctx
    fc = functools.partial(_fwd_chunk, n_lat=n_lat, n_ctx=n_ctx)
    bc = functools.partial(_bwd_chunk, n_lat=n_lat, n_ctx=n_ctx)

    def specs(cm):
        return [pl.BlockSpec((CHUNK, gdw), lambda j, p=p: (cm(j), p)) for p in range(3)] + [
            pl.BlockSpec((CHUNK, LANES), lambda j: (cm(j), 0)),
            pl.BlockSpec((1, LANES, CHUNK), lambda j: (cm(j), 0, 0))]

    blk = 2 * (3 * CHUNK * gdw * 4 + 2 * CHUNK * LANES * 4 + CHUNK * gdw * 4)
    return pl.pallas_call(
        functools.partial(_gdn_body, mh=mh, gh=gh, dk=dk),
        out_shape=(jax.ShapeDtypeStruct((rows, gdw), f32), jax.ShapeDtypeStruct((rows, gdw), f32)),
        grid=(nc,),
        in_specs=specs(fc) + specs(bc),
        out_specs=(pl.BlockSpec((CHUNK, gdw), lambda j: (fc(j), 0)),
                   pl.BlockSpec((CHUNK, gdw), lambda j: (bc(j), 0))),
        scratch_shapes=[pltpu.VMEM((2 * gh, dk, dk), f32)],
        compiler_params=_params(blk, 2 * gh * dk * dk * 4, semantics=("arbitrary",)),
        name="gdn_scan",
    )(qkv, qkv, qkv, gcol, grow, qkv, qkv, qkv, gcol, grow)


def _head_rms(t, g):
    return t * lax.rsqrt(jnp.mean(t * t, axis=-1, keepdims=True) + EPS) * g


def _mix_ml_body(hf_ref, hb_ref, o_gate_ref, g_ref, o_ref, *, mh, dh):
    for h in range(mh):
        sl = slice(h * dh, (h + 1) * dh)
        y = _head_rms(hf_ref[:, sl] + hb_ref[:, sl], g_ref[:, sl])
        o_ref[:, sl] = (_sigmoid(o_gate_ref[:, sl]) * y).astype(bf16)


def _mix_ml(hf, hb, pm, g_row, t, mh, dh, tm):
    mlw = mh * dh
    return pl.pallas_call(
        functools.partial(_mix_ml_body, mh=mh, dh=dh),
        out_shape=jax.ShapeDtypeStruct((t, mlw), bf16),
        grid=(t // tm,),
        in_specs=[pl.BlockSpec((tm, mlw), lambda i: (i, 0)),
                  pl.BlockSpec((tm, mlw), lambda i: (i, 0)),
                  pl.BlockSpec((tm, mlw), lambda i: (i, 3)),
                  pl.BlockSpec((1, mlw), lambda i: (0, 0))],
        out_specs=pl.BlockSpec((tm, mlw), lambda i: (i, 0)),
        compiler_params=_params(4 * tm * mlw * 4, semantics=("arbitrary",)),
        name="mix_ml",
    )(hf, hb, pm, g_row)


def _mix_gd_body(hf_ref, hb_ref, z_ref, g_ref, o_ref, *, gh, dk, ncol):
    for h in range(gh):
        sl = slice(h * dk, (h + 1) * dk)
        y = _head_rms(hf_ref[:, :, sl] + hb_ref[:, :, sl], g_ref[:, sl])
        o_ref[:, :, sl] = jnp.swapaxes(_silu(z_ref[:, :, sl]) * y, 0, 1).astype(bf16)


def _mix_gd(hf3, hb3, pg3, g_row, r, w, gh, dk, rb):
    gdw = gh * dk
    ncol = 16
    return pl.pallas_call(
        functools.partial(_mix_gd_body, gh=gh, dk=dk, ncol=ncol),
        out_shape=jax.ShapeDtypeStruct((r, w, gdw), bf16),
        grid=(w // ncol, r // rb),
        in_specs=[pl.BlockSpec((ncol, rb, gdw), lambda c, j: (c, j, 0)),
                  pl.BlockSpec((ncol, rb, gdw), lambda c, j: (c, j, 0)),
                  pl.BlockSpec((ncol, rb, gdw), lambda c, j: (c, j, 3)),
                  pl.BlockSpec((1, gdw), lambda c, j: (0, 0))],
        out_specs=pl.BlockSpec((rb, ncol, gdw), lambda c, j: (j, c, 0)),
        compiler_params=_params(4 * ncol * rb * gdw * 4, semantics=("arbitrary", "arbitrary")),
        name="mix_gd",
    )(hf3, hb3, pg3, g_row)


def _outproj_body(a1_ref, a2_ref, b1_ref, b2_ref, x_ref, gate_ref, o_ref):
    y = _bdot(a1_ref[...], b1_ref[...]) + _bdot(a2_ref[...], b2_ref[...])
    o_ref[...] = x_ref[...] + gate_ref[0:1, :] * y


def _outproj(a1, a2, w1, w2, x, mod, gate_blk, tm, tn):
    t, k1 = a1.shape
    k2 = a2.shape[1]
    d = x.shape[1]
    tn = min(tn, d)
    gb = gate_blk * (d // tn)
    return pl.pallas_call(
        _outproj_body,
        out_shape=jax.ShapeDtypeStruct((t, d), f32),
        grid=(t // tm, d // tn),
        in_specs=[pl.BlockSpec((tm, k1), lambda i, j: (i, 0)),
                  pl.BlockSpec((tm, k2), lambda i, j: (i, 0)),
                  pl.BlockSpec((k1, tn), lambda i, j: (0, j)),
                  pl.BlockSpec((k2, tn), lambda i, j: (0, j)),
                  pl.BlockSpec((tm, tn), lambda i, j: (i, j)),
                  pl.BlockSpec((8, tn), lambda i, j: (0, gb + j))],
        out_specs=pl.BlockSpec((tm, tn), lambda i, j: (i, j)),
        compiler_params=_params(tm * (k1 + k2) * 2 + (k1 + k2) * tn * 2 + 2 * tm * tn * 4,
                                semantics=("arbitrary", "arbitrary")),
        name="outproj",
    )(a1, a2, w1, w2, x, mod)


def _topk_rows(s, kk, n_rows):
    rid = lax.broadcasted_iota(i32, s.shape, 0)
    vals, idxs = [], []
    for _ in range(kk):
        m = jnp.max(s, axis=0, keepdims=True)
        idx = jnp.min(jnp.where(s == m, rid, n_rows), axis=0, keepdims=True)
        s = jnp.where(rid == idx, -jnp.inf, s)
        vals.append(m)
        idxs.append(idx)
    return vals, idxs


def _peer_topk_body(q_ref, keys_ref, e_ref, g_ref, *, ph, nk, half):
    kk = PEER_TOPK
    tb = q_ref.shape[0]
    e_rows, g_rows = [], []
    for h in range(ph):
        sv, si = [], []
        for p in range(2):
            grp = 2 * h + p
            qg = q_ref[:, grp * half:(grp + 1) * half]
            kg = keys_ref[grp]
            k_hi, k_mid, _ = _split3(kg)
            q_hi, q_mid, _ = _split3(qg)
            nt = lambda a, b: lax.dot_general(a, b, (((1,), (1,)), ((), ())), preferred_element_type=f32)
            s = nt(k_hi, q_hi) + (nt(k_hi, q_mid) + nt(k_mid, q_hi))
            vals, idxs = _topk_rows(s, kk, nk)
            sv.append(vals)
            si.append(idxs)
        pairs = [(a, b) for a in range(kk) for b in range(kk) if (a + 1) * (b + 1) <= kk]
        npad = -len(pairs) % 8
        cand = jnp.concatenate([sv[0][a] + sv[1][b] for a, b in pairs]
                               + [jnp.full((npad, tb), -jnp.inf, f32)], axis=0)
        eidx = jnp.concatenate([si[0][a] * nk + si[1][b] for a, b in pairs]
                               + [jnp.full((npad, tb), -1, i32)], axis=0)
        n_cand = len(pairs) + npad
        rid = lax.broadcasted_iota(i32, cand.shape, 0)
        fv, fe = [], []
        for _ in range(kk):
            m = jnp.max(cand, axis=0, keepdims=True)
            pos = jnp.min(jnp.where(cand == m, rid, n_cand), axis=0, keepdims=True)
            hit = rid == pos
            fe.append(jnp.max(jnp.where(hit, eidx, -1), axis=0, keepdims=True))
            cand = jnp.where(hit, -jnp.inf, cand)
            fv.append(m)
        fv = jnp.concatenate(fv, axis=0)
        ex = jnp.exp(fv - fv[0:1, :])
        g_rows.append(ex / jnp.sum(ex, axis=0, keepdims=True))
        e_rows.append(jnp.concatenate(fe, axis=0))
    e_ref[...] = jnp.concatenate(e_rows, axis=0).T
    g_ref[...] = jnp.concatenate(g_rows, axis=0).T


def _peer_topk(qp, keys, ph, nk, half, tb):
    t = qp.shape[0]
    hk = ph * PEER_TOPK
    return pl.pallas_call(
        functools.partial(_peer_topk_body, ph=ph, nk=nk, half=half),
        out_shape=(jax.ShapeDtypeStruct((t, hk), i32), jax.ShapeDtypeStruct((t, hk), f32)),
        grid=(t // tb,),
        in_specs=[pl.BlockSpec((tb, 2 * ph * half), lambda i: (i, 0)),
                  pl.BlockSpec((2 * ph, nk, half), lambda i: (0, 0, 0))],
        out_specs=(pl.BlockSpec((tb, hk), lambda i: (i, 0)), pl.BlockSpec((tb, hk), lambda i: (i, 0))),
        compiler_params=_params(tb * 2 * ph * half * 4 + 2 * ph * nk * half * 4 + 2 * tb * hk * 4,
                                semantics=("arbitrary",)),
        name="peer_topk",
    )(qp, keys)


def _peer_w_body(e_ref, g_ref, w_ref, *, nk):
    tb, hk = e_ref.shape
    shift = nk.bit_length() - 1
    sub = lax.broadcasted_iota(i32, (nk, hk), 0)

    def group(t8, carry):
        base = pl.multiple_of(t8 * 8, 8)
        e8 = e_ref[pl.ds(base, 8), :]
        g8 = g_ref[pl.ds(base, 8), :]
        ws = []
        for t in range(8):
            e = e8[t:t + 1, :]
            g = g8[t:t + 1, :]
            ei = lax.shift_right_logical(e, shift)
            ej = e & (nk - 1)
            g_hi = g.astype(bf16).astype(f32)
            pt = jnp.where(sub == ei, 1.0, 0.0).astype(bf16)
            hit = sub == ej
            qt = jnp.concatenate([jnp.where(hit, g_hi, 0.0).astype(bf16),
                                  jnp.where(hit, g - g_hi, 0.0).astype(bf16)], axis=1)
            ws.append(lax.dot_general(jnp.concatenate([pt, pt], axis=1), qt, (((1,), (1,)), ((), ())),
                                      preferred_element_type=f32))
        w_ref[:, pl.ds(base, 8), :] = jnp.swapaxes(jnp.stack(ws, axis=0), 0, 1)
        return carry

    lax.fori_loop(0, tb // 8, group, 0)


def _peer_weights(e, g, nk, tb):
    t, hk = e.shape
    return pl.pallas_call(
        functools.partial(_peer_w_body, nk=nk),
        out_shape=jax.ShapeDtypeStruct((nk, t, nk), f32),
        grid=(t // tb,),
        in_specs=[pl.BlockSpec((tb, hk), lambda i: (i, 0)), pl.BlockSpec((tb, hk), lambda i: (i, 0))],
        out_specs=pl.BlockSpec((nk, tb, nk), lambda i: (0, i, 0)),
        compiler_params=_params(tb * nk * nk * 4 + 2 * tb * hk * 4, semantics=("arbitrary",)),
        name="peer_weights",
    )(e, g)


def _gelu(x):
    return 0.5 * x * (1.0 + lax.erf(x * 0.7071067811865476))


def _peer_act_body(f_ref, u_ref, w_ref, o_ref, *, nk, gi):
    acc = _bdot_nt(f_ref[...], u_ref[...])
    for gg in range(gi):
        sl = slice(gg * nk, (gg + 1) * nk)
        o_ref[:, sl] = (w_ref[gg] * _gelu(acc[:, sl])).astype(bf16)


def _peer_act(f, u, w3, nk, tm):
    t, d = f.shape
    ne = u.shape[0]
    gi = 4
    tn = gi * nk
    return pl.pallas_call(
        functools.partial(_peer_act_body, nk=nk, gi=gi),
        out_shape=jax.ShapeDtypeStruct((t, ne), bf16),
        grid=(t // tm, ne // tn),
        in_specs=[pl.BlockSpec((tm, d), lambda i, j: (i, 0)),
                  pl.BlockSpec((tn, d), lambda i, j: (j, 0)),
                  pl.BlockSpec((gi, tm, nk), lambda i, j: (j, i, 0))],
        out_specs=pl.BlockSpec((tm, tn), lambda i, j: (i, j)),
        compiler_params=_params(tm * d * 2 + tn * d * 2 + tm * tn * 4 + tm * tn * 2 + tm * tn * 4,
                                semantics=("arbitrary", "arbitrary")),
        name="peer_act",
    )(f, u, w3)


def _peer_out_body(a_ref, v_ref, o_ref):
    @pl.when(pl.program_id(2) == 0)
    def _():
        o_ref[...] = jnp.zeros_like(o_ref)

    o_ref[...] += jnp.dot(a_ref[...], v_ref[...], preferred_element_type=f32)


def _peer_out(a, v, tm, tn, tk):
    t, ne = a.shape
    d = v.shape[1]
    tn = min(tn, d)
    tk = min(tk, ne)
    return pl.pallas_call(
        _peer_out_body,
        out_shape=jax.ShapeDtypeStruct((t, d), f32),
        grid=(t // tm, d // tn, ne // tk),
        in_specs=[pl.BlockSpec((tm, tk), lambda i, j, k: (i, k)),
                  pl.BlockSpec((tk, tn), lambda i, j, k: (k, j))],
        out_specs=pl.BlockSpec((tm, tn), lambda i, j, k: (i, j)),
        compiler_params=_params(tm * tk * 2 + tk * tn * 2 + 2 * tm * tn * 4,
                                semantics=("arbitrary", "arbitrary", "arbitrary")),
        name="peer_out",
    )(a, v)


def _final_body(x_ref, p_ref, gate_ref, g_ref, o_ref):
    x = x_ref[...] + gate_ref[0:1, :] * p_ref[...]
    o_ref[...] = x * lax.rsqrt(jnp.mean(x * x, axis=-1, keepdims=True) + EPS) * g_ref[...]


def _final(x, p, mod, gate_blk, g, tm):
    t, d = x.shape
    return pl.pallas_call(
        _final_body,
        out_shape=jax.ShapeDtypeStruct((t, d), f32),
        grid=(t // tm,),
        in_specs=[pl.BlockSpec((tm, d), lambda i: (i, 0)), pl.BlockSpec((tm, d), lambda i: (i, 0)),
                  pl.BlockSpec((8, d), lambda i: (0, gate_blk)), pl.BlockSpec((1, d), lambda i: (0, 0))],
        out_specs=pl.BlockSpec((tm, d), lambda i: (i, 0)),
        compiler_params=_params(3 * tm * d * 4, semantics=("arbitrary",)),
        name="final_norm",
    )(x, p, mod, g)


def _pick(n, pref):
    t = min(pref, n)
    while n % t:
        t -= 8
    return t


def kernel(x, c, ctx, c_ctx, w_mod, b_mod, g_mix, w_in, b_gate, gd_conv_w, gd_conv_b, gd_a_log,
           ml_norm_g, gd_norm_g, w_out, g_ffn, peer_wq, peer_keys, peer_u, peer_v, g_final):
    return _block(x, c, ctx, c_ctx, w_mod, b_mod, g_mix, w_in, b_gate, gd_conv_w, gd_conv_b, gd_a_log,
                  ml_norm_g, gd_norm_g, w_out, g_ffn, peer_wq, peer_keys, peer_u, peer_v, g_final)[0]


def _block(x, c, ctx, c_ctx, w_mod, b_mod, g_mix, w_in, b_gate, gd_conv_w, gd_conv_b, gd_a_log,
           ml_norm_g, gd_norm_g, w_out, g_ffn, peer_wq, peer_keys, peer_u, peer_v, g_final):
    assert w_mod.shape[0] == 1 and x.shape[0] == 1, "one layer, one batch element"
    _, t, d = x.shape
    tc = ctx.shape[1]
    _, mh, dh = ml_norm_g.shape
    _, gh, dk = gd_norm_g.shape
    mlw, gdw = mh * dh, gh * dk
    _, ph, _, nk, half = peer_keys.shape
    ne = peer_u.shape[1]
    w = GRID_W
    r = t // w
    assert r == tc and t % CHUNK == 0 and tc % CHUNK == 0 and r % CHUNK == 0
    assert ne == nk * nk and nk & (nk - 1) == 0 and 4 * mh + 4 * gh <= LANES
    n_lat, n_ctx = t // CHUNK, tc // CHUNK

    x2 = x.reshape(t, d)
    ctx2 = ctx.reshape(tc, d)

    s8 = jnp.zeros((8, d), f32).at[0].set(c[0]).at[1].set(c_ctx)
    mod = _mod_vectors(s8, w_mod[0], b_mod)

    h_rm = _normmod(x2, ctx2, g_mix, mod, 0, 1, tm=tc)
    h_cm = _normmod_colmajor(x2.reshape(r, w, d), g_mix[0:1], mod, 0, 1, rb=_pick(r, 64)).reshape(t, d)

    main = 4 * mlw + 4 * gdw
    w_in0 = w_in[0].astype(bf16)
    ng = w_in0.shape[1] - main
    w_gate = jnp.pad(w_in0[:, main:], ((0, 0), (0, LANES - ng)))
    b_gate_row = jnp.pad(b_gate, ((0, 0), (0, LANES - ng)))
    rows = t + tc
    tm_all = _pick(rows, 1280)
    tm_lat = _pick(t, 1024)
    tn_in = _pick(math.gcd(4 * mlw, 4 * gdw), 512)
    gd_off = 4 * mlw // tn_in
    pm = _matmul(h_rm, w_in0, m=rows, tm=tm_all, tn=tn_in, n=4 * mlw, name="proj_ml")
    pg_lat = _matmul(h_cm, w_in0, m=t, tm=tm_lat, tn=tn_in, n=4 * gdw, b_off=gd_off, name="proj_gd_lat")
    pg_ctx = _matmul(h_rm, w_in0, m=tc, tm=tc, tn=tn_in, n=4 * gdw, b_off=gd_off, a_off=t // tc,
                     name="proj_gd_ctx")
    bias_spec = pl.BlockSpec((1, LANES), lambda i, j: (0, 0))
    add_bias = lambda acc, b_ref: acc + b_ref[...]
    g_rm = _matmul(h_rm, w_gate, m=rows, tm=tm_all, tn=LANES, extras=[(b_gate_row, bias_spec)],
                   epilogue=add_bias, name="proj_gate_rm")
    g_cm = _matmul(h_cm, w_gate, m=t, tm=tm_lat, tn=LANES, extras=[(b_gate_row, bias_spec)],
                   epilogue=add_bias, name="proj_gate_cm")

    o_a = 4 * mh + 2 * gh
    alog_row = jnp.zeros((1, LANES), f32).at[0, o_a:o_a + 2 * gh].set(gd_a_log.reshape(-1))
    ml_col, ml_row = _gate_prep(g_rm, n_lat, g_rm, n_lat, n_ctx, alog_row, mh, gh)
    gd_col, gd_row = _gate_prep(g_cm, n_lat, g_rm, n_lat, n_ctx, alog_row, mh, gh)

    hm_f, hm_b = _mlstm_scan(pm, ml_col, ml_row, n_lat, n_ctx, mh, dh)

    qkv = _gdn_conv(pg_lat, pg_ctx, gd_conv_w[0], gd_conv_b, gdw, dk)
    hg_f, hg_b = _gdn_scan(qkv, gd_col, gd_row, n_lat, n_ctx, mh, gh, dk)

    mix_ml = _mix_ml(hm_f, hm_b, pm, ml_norm_g.reshape(1, mlw), t, mh, dh, tm=_pick(t, 256))
    ncols = rows // r
    mix_gd = _mix_gd(hg_f.reshape(ncols, r, gdw), hg_b.reshape(ncols, r, gdw),
                     pg_lat.reshape(w, r, 4 * gdw), gd_norm_g.reshape(1, gdw), r, w, gh, dk,
                     rb=_pick(r, 32)).reshape(t, gdw)
    w_out0 = w_out[0].astype(bf16)
    x1 = _outproj(mix_ml, mix_gd, w_out0[:mlw], w_out0[mlw:], x2, mod, 2, tm=tm_lat, tn=512)

    f_in = _normmod(x1, None, g_ffn, mod, 3, 4, tm=_pick(t, 256))
    qp = _matmul(f_in, peer_wq[0].astype(bf16), m=t, tm=tm_lat, tn=512, name="peer_query")
    e_idx, gates = _peer_topk(qp, peer_keys[0].reshape(2 * ph, nk, half), ph, nk, half, tb=_pick(t, 256))
    w3 = _peer_weights(e_idx, gates, nk, tb=_pick(t, 128))
    act = _peer_act(f_in, peer_u[0].astype(bf16), w3, nk, tm=_pick(t, 1024))
    x2_out = _peer_out(act, peer_v[0].astype(bf16), tm=_pick(t, 1024), tn=2048, tk=1024)

    out = _final(x1, x2_out, mod, 5, g_final.reshape(1, d), tm=_pick(t, 256))
    aux = dict(mod=mod, h_rm=h_rm, h_cm=h_cm, pm=pm, pg_lat=pg_lat, pg_ctx=pg_ctx, g_rm=g_rm, g_cm=g_cm,
               ml_col=ml_col, ml_row=ml_row, gd_col=gd_col, gd_row=gd_row, hm_f=hm_f, hm_b=hm_b, qkv=qkv,
               hg_f=hg_f, hg_b=hg_b, mix_ml=mix_ml, mix_gd=mix_gd, x1=x1, f_in=f_in, qp=qp, e_idx=e_idx,
               gates=gates, w3=w3, act=act, x2_out=x2_out)
    return out.reshape(1, t, d), aux
```

```python
import functools
import math

import jax
import jax.numpy as jnp
from jax import lax
from jax.experimental import pallas as pl
from jax.experimental.pallas import tpu as pltpu

f32 = jnp.float32
bf16 = jnp.bfloat16
i32 = jnp.int32

GRID_W = 64
PEER_TOPK = 16
EPS = 1e-6
NEG = -1e30
CHUNK = 128
LANES = 128
VMEM_CAP = 56 * 1024 * 1024


def _params(block_bytes, scratch_bytes=0, semantics=None):
    est = 2 * block_bytes + scratch_bytes + 16 * 1024 * 1024
    return pltpu.CompilerParams(
        dimension_semantics=semantics,
        vmem_limit_bytes=int(min(max(est, 32 * 1024 * 1024), VMEM_CAP)))


def _bdot(a, b):
    return jnp.dot(a.astype(bf16), b.astype(bf16), preferred_element_type=f32)


def _bdot_nt(a, b):
    return lax.dot_general(a.astype(bf16), b.astype(bf16), (((1,), (1,)), ((), ())),
                           preferred_element_type=f32)


def _bdot_tn(a, b):
    return lax.dot_general(a.astype(bf16), b.astype(bf16), (((0,), (0,)), ((), ())),
                           preferred_element_type=f32)


def _split3(x):
    hi = x.astype(bf16)
    r = x - hi.astype(f32)
    mid = r.astype(bf16)
    lo = (r - mid.astype(f32)).astype(bf16)
    return hi, mid, lo


def _softplus(x):
    return jnp.maximum(x, 0.0) + jnp.log1p(jnp.exp(-jnp.abs(x)))


def _sigmoid(x):
    return 1.0 / (1.0 + jnp.exp(-x))


def _silu(x):
    return x * _sigmoid(x)


def _mod_body(s_ref, w_ref, b_ref, o_ref):
    o_ref[...] = _bdot(_silu(s_ref[...]), w_ref[...]) + b_ref[...]


def _mod_vectors(s8, w_mod, b_mod):
    d, n = w_mod.shape
    tn = min(512, n)
    return pl.pallas_call(
        _mod_body,
        out_shape=jax.ShapeDtypeStruct((8, n), f32),
        grid=(n // tn,),
        in_specs=[pl.BlockSpec((8, d), lambda j: (0, 0)),
                  pl.BlockSpec((d, tn), lambda j: (0, j)),
                  pl.BlockSpec((1, tn), lambda j: (0, j))],
        out_specs=pl.BlockSpec((8, tn), lambda j: (0, j)),
        compiler_params=_params(d * tn * 4 + 8 * d * 4, semantics=("arbitrary",)),
        name="mod_vectors",
    )(s8, w_mod, b_mod)


def _norm_rows(x, g, shift, scale):
    y = x * lax.rsqrt(jnp.mean(x * x, axis=-1, keepdims=True) + EPS)
    return (y * g) * (1.0 + scale) + shift


def _normmod_body(x_ref, c_ref, g_ref, sh_ref, sc_ref, o_ref, *, n_lat):
    is_ctx = pl.program_id(0) >= n_lat
    x = jnp.where(is_ctx, c_ref[...], x_ref[...])
    sh = jnp.where(is_ctx, sh_ref[1:2, :], sh_ref[0:1, :])
    sc = jnp.where(is_ctx, sc_ref[1:2, :], sc_ref[0:1, :])
    o_ref[...] = _norm_rows(x, g_ref[...], sh, sc).astype(bf16)


def _normmod(x, ctx, g, mod, shift_blk, scale_blk, tm):
    t, d = x.shape
    n_lat = t // tm
    if ctx is None:
        ctx = x
        n_ctx = 0
    else:
        n_ctx = ctx.shape[0] // tm
    rows = (n_lat + n_ctx) * tm
    return pl.pallas_call(
        functools.partial(_normmod_body, n_lat=n_lat),
        out_shape=jax.ShapeDtypeStruct((rows, d), bf16),
        grid=(n_lat + n_ctx,),
        in_specs=[pl.BlockSpec((tm, d), lambda i: (jnp.minimum(i, n_lat - 1), 0)),
                  pl.BlockSpec((tm, d), lambda i: (jnp.maximum(i - n_lat, 0), 0)),
                  pl.BlockSpec((1, d), lambda i: (0, 0)),
                  pl.BlockSpec((8, d), lambda i: (0, shift_blk)),
                  pl.BlockSpec((8, d), lambda i: (0, scale_blk))],
        out_specs=pl.BlockSpec((tm, d), lambda i: (i, 0)),
        compiler_params=_params(3 * tm * d * 4, semantics=("arbitrary",)),
        name="normmod",
    )(x, ctx, g, mod, mod)


def _normmod_cm_body(x_ref, g_ref, sh_ref, sc_ref, o_ref, *, ncol):
    x = jnp.swapaxes(x_ref[...], 0, 1)
    o_ref[...] = _norm_rows(x, g_ref[...], sh_ref[0:1, :], sc_ref[0:1, :]).astype(bf16)


def _normmod_colmajor(x3, g, mod, shift_blk, scale_blk, rb):
    r, w, d = x3.shape
    ncol = 8
    return pl.pallas_call(
        functools.partial(_normmod_cm_body, ncol=ncol),
        out_shape=jax.ShapeDtypeStruct((w, r, d), bf16),
        grid=(w // ncol, r // rb),
        in_specs=[pl.BlockSpec((rb, ncol, d), lambda c, j: (j, c, 0)),
                  pl.BlockSpec((1, d), lambda c, j: (0, 0)),
                  pl.BlockSpec((8, d), lambda c, j: (0, shift_blk)),
                  pl.BlockSpec((8, d), lambda c, j: (0, scale_blk))],
        out_specs=pl.BlockSpec((ncol, rb, d), lambda c, j: (c, j, 0)),
        compiler_params=_params(2 * rb * ncol * d * 4, semantics=("arbitrary", "arbitrary")),
        name="normmod_colmajor",
    )(x3, g, mod, mod)


def _mm_body(a_ref, b_ref, *rest, nt, epilogue):
    acc = (_bdot_nt if nt else _bdot)(a_ref[...], b_ref[...])
    o_ref = rest[-1]
    o_ref[...] = epilogue(acc, *rest[:-1]).astype(o_ref.dtype)


def _matmul(a, b, *, m, tm, tn, nt=False, a_off=0, n=None, b_off=0, extras=(), epilogue=None,
            out_dtype=f32, name="matmul"):
    k = a.shape[1]
    if n is None:
        n = b.shape[0] if nt else b.shape[1]
    tn = min(tn, n)
    if epilogue is None:
        epilogue = lambda acc: acc
    b_spec = (pl.BlockSpec((tn, k), lambda i, j: (j, 0)) if nt
              else pl.BlockSpec((k, tn), lambda i, j: (0, j + b_off)))
    extra_bytes = sum(jnp.dtype(e.dtype).itemsize * math.prod(s.block_shape) for e, s in extras)
    return pl.pallas_call(
        functools.partial(_mm_body, nt=nt, epilogue=epilogue),
        out_shape=jax.ShapeDtypeStruct((m, n), out_dtype),
        grid=(m // tm, n // tn),
        in_specs=[pl.BlockSpec((tm, k), lambda i, j: (i + a_off, 0)), b_spec] + [s for _, s in extras],
        out_specs=pl.BlockSpec((tm, tn), lambda i, j: (i, j)),
        compiler_params=_params(tm * k * 2 + k * tn * 2 + tm * tn * 4 + extra_bytes,
                                semantics=("arbitrary", "arbitrary")),
        name=name,
    )(a, b, *[e for e, _ in extras])


def _gate_body(ga_ref, gb_ref, alog_ref, col_ref, row_ref, *, n_a, mh, gh):
    o_f, o_b, o_a = 2 * mh, 4 * mh, 4 * mh + 2 * gh
    is_b = pl.program_id(0) >= n_a
    g = jnp.where(is_b, gb_ref[...], ga_ref[...])
    n = g.shape[0]
    lane = lax.broadcasted_iota(i32, (n, LANES), 1)
    lf = -_softplus(-g)
    gl = -jnp.exp(alog_ref[...]) * _softplus(g)
    src = jnp.where(lane < o_b, lf, gl)
    rid = lax.broadcasted_iota(i32, (n, n), 0)
    cid = lax.broadcasted_iota(i32, (n, n), 1)
    tri_f = jnp.where(cid <= rid, 1.0, 0.0).astype(bf16)
    tri_b = jnp.where(cid >= rid, 1.0, 0.0).astype(bf16)
    parts = _split3(src)
    cf = sum(jnp.dot(tri_f, p, preferred_element_type=f32) for p in parts)
    cb = sum(jnp.dot(tri_b, p, preferred_element_type=f32) for p in parts)
    fwd = ((lane >= o_f) & (lane < o_f + mh)) | ((lane >= o_a) & (lane < o_a + gh))
    bwd = ((lane >= o_f + mh) & (lane < o_b)) | ((lane >= o_a + gh) & (lane < o_a + 2 * gh))
    beta = (lane >= o_b) & (lane < o_a)
    out = jnp.where(lane < o_f, g,
                    jnp.where(fwd, cf, jnp.where(bwd, cb, jnp.where(beta, _sigmoid(g), 0.0))))
    col_ref[...] = out
    row_ref[0] = out.T


def _gate_prep(g_a, n_a, g_b, b_off, n_b, alog_row, mh, gh):
    nc = n_a + n_b
    return pl.pallas_call(
        functools.partial(_gate_body, n_a=n_a, mh=mh, gh=gh),
        out_shape=(jax.ShapeDtypeStruct((nc * CHUNK, LANES), f32),
                   jax.ShapeDtypeStruct((nc, LANES, CHUNK), f32)),
        grid=(nc,),
        in_specs=[pl.BlockSpec((CHUNK, LANES), lambda i: (jnp.minimum(i, n_a - 1), 0)),
                  pl.BlockSpec((CHUNK, LANES), lambda i: (jnp.maximum(i - n_a, 0) + b_off, 0)),
                  pl.BlockSpec((1, LANES), lambda i: (0, 0))],
        out_specs=(pl.BlockSpec((CHUNK, LANES), lambda i: (i, 0)),
                   pl.BlockSpec((1, LANES, CHUNK), lambda i: (i, 0, 0))),
        compiler_params=_params(4 * CHUNK * LANES * 4, semantics=("arbitrary",)),
        name="gate_prep",
    )(g_a, g_b, alog_row)


def _fwd_chunk(j, n_lat, n_ctx):
    return jnp.where(j < n_ctx, n_lat + j, j - n_ctx)


def _bwd_chunk(j, n_lat, n_ctx):
    return n_lat + n_ctx - 1 - j


def _tri_masks(n):
    rid = lax.broadcasted_iota(i32, (n, n), 0)
    cid = lax.broadcasted_iota(i32, (n, n), 1)
    return cid <= rid, cid >= rid, cid < rid, cid > rid


def _mlstm_body(qf_ref, kf_ref, vf_ref, cf_ref, rf_ref, qb_ref, kb_ref, vb_ref, cb_ref, rb_ref,
                hf_ref, hb_ref, c_scr, n_scr, m_scr, *, mh, dh):
    @pl.when(pl.program_id(0) == 0)
    def _():
        c_scr[...] = jnp.zeros_like(c_scr)
        n_scr[...] = jnp.zeros_like(n_scr)
        m_scr[...] = jnp.full_like(m_scr, NEG)

    n = CHUNK
    incl_f, incl_b, _, _ = _tri_masks(n)
    scale = dh ** -0.5
    dirs = ((0, qf_ref, kf_ref, vf_ref, cf_ref, rf_ref, hf_ref, incl_f),
            (1, qb_ref, kb_ref, vb_ref, cb_ref, rb_ref, hb_ref, incl_b))
    chains = []
    for d, q_ref, k_ref, v_ref, col_ref, row_ref, h_ref, incl in dirs:
        for h in range(mh):
            ch = d * mh + h
            li, lb = d * mh + h, 2 * mh + d * mh + h
            sl = slice(h * dh, (h + 1) * dh)
            brow = row_ref[0, lb:lb + 1, :]
            chains.append(dict(
                ch=ch, sl=sl, h_ref=h_ref, incl=incl,
                q=q_ref[:, sl] * scale, k=k_ref[:, sl], v=v_ref[:, sl],
                icol=col_ref[:, li:li + 1], bcol=col_ref[:, lb:lb + 1],
                irow=row_ref[0, li:li + 1, :], brow=brow,
                btot=brow[:, n - 1:n] if d == 0 else brow[:, 0:1],
                m_st=m_scr[ch, :, 0:1], c_st=c_scr[ch], n_st=n_scr[ch]))
    reps = dh // LANES
    ones = jnp.ones((n, LANES), bf16)
    wide = lambda x: jnp.concatenate([x] * reps, axis=1)
    for c in chains:
        bcol_bc = jnp.broadcast_to(c["bcol"], (n, LANES))
        dmat = jnp.where(c["incl"], bcol_bc - c["brow"] + c["irow"], NEG)
        m_row = jnp.maximum(c["bcol"] + c["m_st"], jnp.max(dmat, axis=1, keepdims=True))
        c["m_row"] = jnp.broadcast_to(m_row, (n, LANES))
        c["p"] = jnp.exp(dmat - c["m_row"])
        c["w_inter"] = jnp.exp(bcol_bc + c["m_st"] - c["m_row"])
    for c in chains:
        c["s"] = _bdot_nt(c["q"], c["k"]) * c["p"]
        c["qc"] = _bdot(c["q"], c["c_st"])
        c["qn"] = _bdot(c["q"], c["n_st"])
    for c in chains:
        s_hi = c["s"].astype(bf16)
        s_lo = (c["s"] - s_hi.astype(f32)).astype(bf16)
        row_sum = (jnp.dot(s_hi, ones, preferred_element_type=f32)
                   + jnp.dot(s_lo, ones, preferred_element_type=f32))
        den = c["w_inter"] * c["qn"] + row_sum
        inv = 1.0 / jnp.maximum(jnp.abs(den), jnp.exp(-c["m_row"]))
        num = wide(c["w_inter"]) * c["qc"] + _bdot(c["s"], c["v"])
        c["h_ref"][:, c["sl"]] = num * wide(inv)
    eye = jnp.where(lax.broadcasted_iota(i32, (dh, dh), 0) == lax.broadcasted_iota(i32, (dh, dh), 1),
                    1.0, 0.0).astype(bf16)
    for c in chains:
        ch = c["ch"]
        wlog = c["btot"] - c["bcol"] + c["icol"]
        m_new = jnp.maximum(c["btot"] + c["m_st"], jnp.max(wlog, axis=0, keepdims=True))
        a_st = jnp.exp(c["btot"] + c["m_st"] - m_new)
        kw = c["k"] * wide(jnp.broadcast_to(jnp.exp(wlog - m_new), (n, LANES)))
        kw_t = _bdot_nt(eye, kw)
        c_scr[ch] = a_st * c["c_st"] + _bdot(kw_t, c["v"])
        n_scr[ch] = a_st * c["n_st"] + _bdot(kw_t, ones)
        m_scr[ch] = jnp.broadcast_to(m_new, (1, LANES))


def _mlstm_scan(pm, gcol, grow, n_lat, n_ctx, mh, dh):
    rows = pm.shape[0]
    mlw = mh * dh
    nc = n_lat + n_ctx
    fc = functools.partial(_fwd_chunk, n_lat=n_lat, n_ctx=n_ctx)
    bc = functools.partial(_bwd_chunk, n_lat=n_lat, n_ctx=n_ctx)

    def specs(cm):
        return [pl.BlockSpec((CHUNK, mlw), lambda j, p=p: (cm(j), p)) for p in range(3)] + [
            pl.BlockSpec((CHUNK, LANES), lambda j: (cm(j), 0)),
            pl.BlockSpec((1, LANES, CHUNK), lambda j: (cm(j), 0, 0))]

    blk = 2 * (3 * CHUNK * mlw * 4 + 2 * CHUNK * LANES * 4 + CHUNK * mlw * 4)
    scr = 2 * mh * (dh * dh + dh * LANES + 8 * LANES) * 4
    return pl.pallas_call(
        functools.partial(_mlstm_body, mh=mh, dh=dh),
        out_shape=(jax.ShapeDtypeStruct((rows, mlw), f32), jax.ShapeDtypeStruct((rows, mlw), f32)),
        grid=(nc,),
        in_specs=specs(fc) + specs(bc),
        out_specs=(pl.BlockSpec((CHUNK, mlw), lambda j: (fc(j), 0)),
                   pl.BlockSpec((CHUNK, mlw), lambda j: (bc(j), 0))),
        scratch_shapes=[pltpu.VMEM((2 * mh, dh, dh), f32),
                        pltpu.VMEM((2 * mh, dh, LANES), f32),
                        pltpu.VMEM((2 * mh, 1, LANES), f32)],
        compiler_params=_params(blk, scr, semantics=("arbitrary",)),
        name="mlstm_scan",
    )(pm, pm, pm, gcol, grow, pm, pm, pm, gcol, grow)


def _gdn_conv_body(l_ref, c_ref, p_ref, n_ref, w_ref, b_ref, o_ref, ext, *, n_lat, kw, gdw, dk):
    b = pl.program_id(0)
    cw = o_ref.shape[1]
    rb = o_ref.shape[0]
    is_ctx = b >= n_lat
    half = kw // 2
    ext[0:8, :] = jnp.where((b == 0) | is_ctx, 0.0, p_ref[...])
    ext[8:8 + rb, :] = jnp.where(is_ctx, c_ref[...], l_ref[...])
    ext[8 + rb:16 + rb, :] = jnp.where(b >= n_lat - 1, 0.0, n_ref[...])
    y = jnp.broadcast_to(b_ref[...], (rb, cw))
    for j in range(kw):
        y = y + ext[pl.ds(8 - half + j, rb), :] * w_ref[j:j + 1, :]
    y = _silu(y)
    part = (pl.program_id(1) * cw) // gdw
    qk_scale = jnp.where(part == 0, dk ** -0.5, 1.0)
    for hh in range(cw // dk):
        sl = slice(hh * dk, (hh + 1) * dk)
        t = y[:, sl]
        tn = t * lax.rsqrt(jnp.sum(t * t, axis=-1, keepdims=True) + EPS) * qk_scale
        o_ref[:, sl] = jnp.where(part == 2, t, tn)


def _gdn_conv(pg_lat, pg_ctx, conv_w, conv_b, gdw, dk):
    t = pg_lat.shape[0]
    rb = pg_ctx.shape[0]
    n_lat = t // rb
    kw = conv_w.shape[0]
    cw = gdw
    nh8 = rb // 8
    return pl.pallas_call(
        functools.partial(_gdn_conv_body, n_lat=n_lat, kw=kw, gdw=gdw, dk=dk),
        out_shape=jax.ShapeDtypeStruct((t + rb, 3 * gdw), f32),
        grid=(n_lat + 1, 3 * gdw // cw),
        in_specs=[pl.BlockSpec((rb, cw), lambda b, c: (jnp.minimum(b, n_lat - 1), c)),
                  pl.BlockSpec((rb, cw), lambda b, c: (0, c)),
                  pl.BlockSpec((8, cw), lambda b, c: (jnp.clip(b * nh8 - 1, 0, n_lat * nh8 - 1), c)),
                  pl.BlockSpec((8, cw), lambda b, c: (jnp.minimum((b + 1) * nh8, n_lat * nh8 - 1), c)),
                  pl.BlockSpec((kw, cw), lambda b, c: (0, c)),
                  pl.BlockSpec((1, cw), lambda b, c: (0, c))],
        out_specs=pl.BlockSpec((rb, cw), lambda b, c: (b, c)),
        scratch_shapes=[pltpu.VMEM((rb + 16, cw), f32)],
        compiler_params=_params(4 * rb * cw * 4, (rb + 16) * cw * 4, semantics=("arbitrary", "arbitrary")),
        name="gdn_conv",
    )(pg_lat, pg_ctx, pg_lat, pg_lat, conv_w, conv_b)


def _gdn_body(qf_ref, kf_ref, vf_ref, cf_ref, rf_ref, qb_ref, kb_ref, vb_ref, cb_ref, rb_ref,
              of_ref, ob_ref, s_scr, *, mh, gh, dk):
    @pl.when(pl.program_id(0) == 0)
    def _():
        s_scr[...] = jnp.zeros_like(s_scr)

    n = CHUNK
    incl_f, incl_b, strict_f, strict_b = _tri_masks(n)
    rid = lax.broadcasted_iota(i32, (n, n), 0)
    cid = lax.broadcasted_iota(i32, (n, n), 1)
    o_b, o_a = 4 * mh, 4 * mh + 2 * gh
    dirs = ((0, qf_ref, kf_ref, vf_ref, cf_ref, rf_ref, of_ref, incl_f, strict_f),
            (1, qb_ref, kb_ref, vb_ref, cb_ref, rb_ref, ob_ref, incl_b, strict_b))
    chains = []
    for d, q_ref, k_ref, v_ref, col_ref, row_ref, o_ref, incl, strict in dirs:
        for h in range(gh):
            lbeta, lg = o_b + d * gh + h, o_a + d * gh + h
            sl = slice(h * dk, (h + 1) * dk)
            grow = row_ref[0, lg:lg + 1, :]
            chains.append(dict(
                ch=d * gh + h, sl=sl, o_ref=o_ref, incl=incl, strict=strict,
                q=q_ref[:, sl], k=k_ref[:, sl], v=v_ref[:, sl],
                beta=col_ref[:, lbeta:lbeta + 1], gcol=col_ref[:, lg:lg + 1], grow=grow,
                glast=grow[:, n - 1:n] if d == 0 else grow[:, 0:1]))
    for c in chains:
        c["decay"] = jnp.exp(jnp.where(c["incl"], c["gcol"] - c["grow"], NEG))
        c["kb"] = c["k"] * c["beta"]
        c["egc"] = jnp.exp(c["gcol"])
    for c in chains:
        c["a"] = jnp.where(c["strict"], _bdot_nt(c["kb"], c["k"]) * c["decay"], 0.0)
        c["attn"] = _bdot_nt(c["q"], c["k"]) * c["decay"]
    eye = jnp.where(rid == cid, 1.0, 0.0)
    for c in chains:
        c["t"] = eye - jnp.where((rid >> 1) == (cid >> 1), c["a"], 0.0)
    sb = 1
    while (1 << sb) < n:
        off = ((rid >> (sb + 1)) == (cid >> (sb + 1))) & ((rid >> sb) != (cid >> sb))
        for c in chains:
            c["x"] = _bdot(jnp.where(off, c["a"], 0.0), c["t"])
        for c in chains:
            c["t"] = c["t"] - _bdot(c["t"], c["x"])
        sb += 1
    for c in chains:
        c["w"] = _bdot(c["t"], c["kb"] * c["egc"])
        c["u"] = _bdot(c["t"], c["v"] * c["beta"])
        c["s"] = s_scr[c["ch"]]
    for c in chains:
        c["v_new"] = c["u"] - _bdot(c["w"], c["s"])
        c["o"] = _bdot(c["q"] * c["egc"], c["s"])
    for c in chains:
        c["o_ref"][:, c["sl"]] = c["o"] + _bdot(c["attn"], c["v_new"])
        kd = c["k"] * jnp.exp(c["glast"] - c["gcol"])
        s_scr[c["ch"]] = jnp.exp(c["glast"]) * c["s"] + _bdot_tn(kd, c["v_new"])


def _gdn_scan(qkv, gcol, grow, n_lat, n_ctx, mh, gh, dk):
    rows = qkv.shape[0]
    gdw = gh * dk
    nc = n_lat + n_ctx
    fc = functools.partial(_fwd_chunk, n_lat=n_lat, n_ctx=n_ctx)
    bc = functools.partial(_bwd_chunk, n_lat=n_lat, n_ctx=n_ctx)

    def specs(cm):
        return [pl.BlockSpec((CHUNK, gdw), lambda j, p=p: (cm(j), p)) for p in range(3)] + [
            pl.BlockSpec((CHUNK, LANES), lambda j: (cm(j), 0)),
            pl.BlockSpec((1, LANES, CHUNK), lambda j: (cm(j), 0, 0))]

    blk = 2 * (3 * CHUNK * gdw * 4 + 2 * CHUNK * LANES * 4 + CHUNK * gdw * 4)
    return pl.pallas_call(
        functools.partial(_gdn_body, mh=mh, gh=gh, dk=dk),
        out_shape=(jax.ShapeDtypeStruct((rows, gdw), f32), jax.ShapeDtypeStruct((rows, gdw), f32)),
        grid=(nc,),
        in_specs=specs(fc) + specs(bc),
        out_specs=(pl.BlockSpec((CHUNK, gdw), lambda j: (fc(j), 0)),
                   pl.BlockSpec((CHUNK, gdw), lambda j: (bc(j), 0))),
        scratch_shapes=[pltpu.VMEM((2 * gh, dk, dk), f32)],
        compiler_params=_params(blk, 2 * gh * dk * dk * 4, semantics=("arbitrary",)),
        name="gdn_scan",
    )(qkv, qkv, qkv, gcol, grow, qkv, qkv, qkv, gcol, grow)


def _head_rms(t, g):
    return t * lax.rsqrt(jnp.mean(t * t, axis=-1, keepdims=True) + EPS) * g


def _mix_ml_body(hf_ref, hb_ref, o_gate_ref, g_ref, o_ref, *, mh, dh):
    for h in range(mh):
        sl = slice(h * dh, (h + 1) * dh)
        y = _head_rms(hf_ref[:, sl] + hb_ref[:, sl], g_ref[:, sl])
        o_ref[:, sl] = (_sigmoid(o_gate_ref[:, sl]) * y).astype(bf16)


def _mix_ml(hf, hb, pm, g_row, t, mh, dh, tm):
    mlw = mh * dh
    return pl.pallas_call(
        functools.partial(_mix_ml_body, mh=mh, dh=dh),
        out_shape=jax.ShapeDtypeStruct((t, mlw), bf16),
        grid=(t // tm,),
        in_specs=[pl.BlockSpec((tm, mlw), lambda i: (i, 0)),
                  pl.BlockSpec((tm, mlw), lambda i: (i, 0)),
                  pl.BlockSpec((tm, mlw), lambda i: (i, 3)),
                  pl.BlockSpec((1, mlw), lambda i: (0, 0))],
        out_specs=pl.BlockSpec((tm, mlw), lambda i: (i, 0)),
        compiler_params=_params(4 * tm * mlw * 4, semantics=("arbitrary",)),
        name="mix_ml",
    )(hf, hb, pm, g_row)


def _mix_gd_body(hf_ref, hb_ref, z_ref, g_ref, o_ref, *, gh, dk, ncol):
    for h in range(gh):
        sl = slice(h * dk, (h + 1) * dk)
        y = _head_rms(hf_ref[:, :, sl] + hb_ref[:, :, sl], g_ref[:, sl])
        o_ref[:, :, sl] = jnp.swapaxes(_silu(z_ref[:, :, sl]) * y, 0, 1).astype(bf16)


def _mix_gd(hf3, hb3, pg3, g_row, r, w, gh, dk, rb):
    gdw = gh * dk
    ncol = 16
    return pl.pallas_call(
        functools.partial(_mix_gd_body, gh=gh, dk=dk, ncol=ncol),
        out_shape=jax.ShapeDtypeStruct((r, w, gdw), bf16),
        grid=(w // ncol, r // rb),
        in_specs=[pl.BlockSpec((ncol, rb, gdw), lambda c, j: (c, j, 0)),
                  pl.BlockSpec((ncol, rb, gdw), lambda c, j: (c, j, 0)),
                  pl.BlockSpec((ncol, rb, gdw), lambda c, j: (c, j, 3)),
                  pl.BlockSpec((1, gdw), lambda c, j: (0, 0))],
        out_specs=pl.BlockSpec((rb, ncol, gdw), lambda c, j: (j, c, 0)),
        compiler_params=_params(4 * ncol * rb * gdw * 4, semantics=("arbitrary", "arbitrary")),
        name="mix_gd",
    )(hf3, hb3, pg3, g_row)


def _outproj_body(a1_ref, a2_ref, b1_ref, b2_ref, x_ref, gate_ref, o_ref):
    y = _bdot(a1_ref[...], b1_ref[...]) + _bdot(a2_ref[...], b2_ref[...])
    o_ref[...] = x_ref[...] + gate_ref[0:1, :] * y


def _outproj(a1, a2, w1, w2, x, mod, gate_blk, tm, tn):
    t, k1 = a1.shape
    k2 = a2.shape[1]
    d = x.shape[1]
    tn = min(tn, d)
    gb = gate_blk * (d // tn)
    return pl.pallas_call(
        _outproj_body,
        out_shape=jax.ShapeDtypeStruct((t, d), f32),
        grid=(t // tm, d // tn),
        in_specs=[pl.BlockSpec((tm, k1), lambda i, j: (i, 0)),
                  pl.BlockSpec((tm, k2), lambda i, j: (i, 0)),
                  pl.BlockSpec((k1, tn), lambda i, j: (0, j)),
                  pl.BlockSpec((k2, tn), lambda i, j: (0, j)),
                  pl.BlockSpec((tm, tn), lambda i, j: (i, j)),
                  pl.BlockSpec((8, tn), lambda i, j: (0, gb + j))],
        out_specs=pl.BlockSpec((tm, tn), lambda i, j: (i, j)),
        compiler_params=_params(tm * (k1 + k2) * 2 + (k1 + k2) * tn * 2 + 2 * tm * tn * 4,
                                semantics=("arbitrary", "arbitrary")),
        name="outproj",
    )(a1, a2, w1, w2, x, mod)


def _topk_rows(s, kk, n_rows):
    rid = lax.broadcasted_iota(i32, s.shape, 0)
    vals, idxs = [], []
    for _ in range(kk):
        m = jnp.max(s, axis=0, keepdims=True)
        idx = jnp.min(jnp.where(s == m, rid, n_rows), axis=0, keepdims=True)
        s = jnp.where(rid == idx, -jnp.inf, s)
        vals.append(m)
        idxs.append(idx)
    return vals, idxs


def _peer_topk_body(q_ref, keys_ref, e_ref, g_ref, *, ph, nk, half):
    kk = PEER_TOPK
    tb = q_ref.shape[0]
    e_rows, g_rows = [], []
    for h in range(ph):
        sv, si = [], []
        for p in range(2):
            grp = 2 * h + p
            qg = q_ref[:, grp * half:(grp + 1) * half]
            kg = keys_ref[grp]
            k_hi, k_mid, _ = _split3(kg)
            q_hi, q_mid, _ = _split3(qg)
            nt = lambda a, b: lax.dot_general(a, b, (((1,), (1,)), ((), ())), preferred_element_type=f32)
            s = nt(k_hi, q_hi) + (nt(k_hi, q_mid) + nt(k_mid, q_hi))
            vals, idxs = _topk_rows(s, kk, nk)
            sv.append(vals)
            si.append(idxs)
        pairs = [(a, b) for a in range(kk) for b in range(kk) if (a + 1) * (b + 1) <= kk]
        npad = -len(pairs) % 8
        cand = jnp.concatenate([sv[0][a] + sv[1][b] for a, b in pairs]
                               + [jnp.full((npad, tb), -jnp.inf, f32)], axis=0)
        eidx = jnp.concatenate([si[0][a] * nk + si[1][b] for a, b in pairs]
                               + [jnp.full((npad, tb), -1, i32)], axis=0)
        n_cand = len(pairs) + npad
        rid = lax.broadcasted_iota(i32, cand.shape, 0)
        fv, fe = [], []
        for _ in range(kk):
            m = jnp.max(cand, axis=0, keepdims=True)
            pos = jnp.min(jnp.where(cand == m, rid, n_cand), axis=0, keepdims=True)
            hit = rid == pos
            fe.append(jnp.max(jnp.where(hit, eidx, -1), axis=0, keepdims=True))
            cand = jnp.where(hit, -jnp.inf, cand)
            fv.append(m)
        fv = jnp.concatenate(fv, axis=0)
        ex = jnp.exp(fv - fv[0:1, :])
        g_rows.append(ex / jnp.sum(ex, axis=0, keepdims=True))
        e_rows.append(jnp.concatenate(fe, axis=0))
    e_ref[...] = jnp.concatenate(e_rows, axis=0).T
    g_ref[...] = jnp.concatenate(g_rows, axis=0).T


def _peer_topk(qp, keys, ph, nk, half, tb):
    t = qp.shape[0]
    hk = ph * PEER_TOPK
    return pl.pallas_call(
        functools.partial(_peer_topk_body, ph=ph, nk=nk, half=half),
        out_shape=(jax.ShapeDtypeStruct((t, hk), i32), jax.ShapeDtypeStruct((t, hk), f32)),
        grid=(t // tb,),
        in_specs=[pl.BlockSpec((tb, 2 * ph * half), lambda i: (i, 0)),
                  pl.BlockSpec((2 * ph, nk, half), lambda i: (0, 0, 0))],
        out_specs=(pl.BlockSpec((tb, hk), lambda i: (i, 0)), pl.BlockSpec((tb, hk), lambda i: (i, 0))),
        compiler_params=_params(tb * 2 * ph * half * 4 + 2 * ph * nk * half * 4 + 2 * tb * hk * 4,
                                semantics=("arbitrary",)),
        name="peer_topk",
    )(qp, keys)


def _peer_w_body(e_ref, g_ref, w_ref, *, nk):
    tb, hk = e_ref.shape
    shift = nk.bit_length() - 1
    sub = lax.broadcasted_iota(i32, (nk, hk), 0)

    def group(t8, carry):
        base = pl.multiple_of(t8 * 8, 8)
        e8 = e_ref[pl.ds(base, 8), :]
        g8 = g_ref[pl.ds(base, 8), :]
        ws = []
        for t in range(8):
            e = e8[t:t + 1, :]
            g = g8[t:t + 1, :]
            ei = lax.shift_right_logical(e, shift)
            ej = e & (nk - 1)
            g_hi = g.astype(bf16).astype(f32)
            pt = jnp.where(sub == ei, 1.0, 0.0).astype(bf16)
            hit = sub == ej
            qt = jnp.concatenate([jnp.where(hit, g_hi, 0.0).astype(bf16),
                                  jnp.where(hit, g - g_hi, 0.0).astype(bf16)], axis=1)
            ws.append(lax.dot_general(jnp.concatenate([pt, pt], axis=1), qt, (((1,), (1,)), ((), ())),
                                      preferred_element_type=f32))
        w_ref[:, pl.ds(base, 8), :] = jnp.swapaxes(jnp.stack(ws, axis=0), 0, 1)
        return carry

    lax.fori_loop(0, tb // 8, group, 0, unroll=4)


def _peer_weights(e, g, nk, tb):
    t, hk = e.shape
    return pl.pallas_call(
        functools.partial(_peer_w_body, nk=nk),
        out_shape=jax.ShapeDtypeStruct((nk, t, nk), f32),
        grid=(t // tb,),
        in_specs=[pl.BlockSpec((tb, hk), lambda i: (i, 0)), pl.BlockSpec((tb, hk), lambda i: (i, 0))],
        out_specs=pl.BlockSpec((nk, tb, nk), lambda i: (0, i, 0)),
        compiler_params=_params(tb * nk * nk * 4 + 2 * tb * hk * 4, semantics=("arbitrary",)),
        name="peer_weights",
    )(e, g)


def _gelu(x):
    return 0.5 * x * (1.0 + lax.erf(x * 0.7071067811865476))


def _peer_act_body(f_ref, u_ref, w_ref, o_ref, *, nk, gi):
    acc = _bdot_nt(f_ref[...], u_ref[...])
    for gg in range(gi):
        sl = slice(gg * nk, (gg + 1) * nk)
        o_ref[:, sl] = (w_ref[gg] * _gelu(acc[:, sl])).astype(bf16)


def _peer_act(f, u, w3, nk, tm):
    t, d = f.shape
    ne = u.shape[0]
    gi = 4
    tn = gi * nk
    return pl.pallas_call(
        functools.partial(_peer_act_body, nk=nk, gi=gi),
        out_shape=jax.ShapeDtypeStruct((t, ne), bf16),
        grid=(t // tm, ne // tn),
        in_specs=[pl.BlockSpec((tm, d), lambda i, j: (i, 0)),
                  pl.BlockSpec((tn, d), lambda i, j: (j, 0)),
                  pl.BlockSpec((gi, tm, nk), lambda i, j: (j, i, 0))],
        out_specs=pl.BlockSpec((tm, tn), lambda i, j: (i, j)),
        compiler_params=_params(tm * d * 2 + tn * d * 2 + tm * tn * 4 + tm * tn * 2 + tm * tn * 4,
                                semantics=("arbitrary", "arbitrary")),
        name="peer_act",
    )(f, u, w3)


def _peer_out_body(a_ref, v_ref, o_ref):
    @pl.when(pl.program_id(2) == 0)
    def _():
        o_ref[...] = jnp.zeros_like(o_ref)

    o_ref[...] += jnp.dot(a_ref[...], v_ref[...], preferred_element_type=f32)


def _peer_out(a, v, tm, tn, tk):
    t, ne = a.shape
    d = v.shape[1]
    tn = min(tn, d)
    tk = min(tk, ne)
    return pl.pallas_call(
        _peer_out_body,
        out_shape=jax.ShapeDtypeStruct((t, d), f32),
        grid=(t // tm, d // tn, ne // tk),
        in_specs=[pl.BlockSpec((tm, tk), lambda i, j, k: (i, k)),
                  pl.BlockSpec((tk, tn), lambda i, j, k: (k, j))],
        out_specs=pl.BlockSpec((tm, tn), lambda i, j, k: (i, j)),
        compiler_params=_params(tm * tk * 2 + tk * tn * 2 + 2 * tm * tn * 4,
                                semantics=("arbitrary", "arbitrary", "arbitrary")),
        name="peer_out",
    )(a, v)


def _final_body(x_ref, p_ref, gate_ref, g_ref, o_ref):
    x = x_ref[...] + gate_ref[0:1, :] * p_ref[...]
    o_ref[...] = x * lax.rsqrt(jnp.mean(x * x, axis=-1, keepdims=True) + EPS) * g_ref[...]


def _final(x, p, mod, gate_blk, g, tm):
    t, d = x.shape
    return pl.pallas_call(
        _final_body,
        out_shape=jax.ShapeDtypeStruct((t, d), f32),
        grid=(t // tm,),
        in_specs=[pl.BlockSpec((tm, d), lambda i: (i, 0)), pl.BlockSpec((tm, d), lambda i: (i, 0)),
                  pl.BlockSpec((8, d), lambda i: (0, gate_blk)), pl.BlockSpec((1, d), lambda i: (0, 0))],
        out_specs=pl.BlockSpec((tm, d), lambda i: (i, 0)),
        compiler_params=_params(3 * tm * d * 4, semantics=("arbitrary",)),
        name="final_norm",
    )(x, p, mod, g)


def _pick(n, pref):
    t = min(pref, n)
    while n % t:
        t -= 8
    return t


def kernel(x, c, ctx, c_ctx, w_mod, b_mod, g_mix, w_in, b_gate, gd_conv_w, gd_conv_b, gd_a_log,
           ml_norm_g, gd_norm_g, w_out, g_ffn, peer_wq, peer_keys, peer_u, peer_v, g_final):
    return _block(x, c, ctx, c_ctx, w_mod, b_mod, g_mix, w_in, b_gate, gd_conv_w, gd_conv_b, gd_a_log,
                  ml_norm_g, gd_norm_g, w_out, g_ffn, peer_wq, peer_keys, peer_u, peer_v, g_final)[0]


def _block(x, c, ctx, c_ctx, w_mod, b_mod, g_mix, w_in, b_gate, gd_conv_w, gd_conv_b, gd_a_log,
           ml_norm_g, gd_norm_g, w_out, g_ffn, peer_wq, peer_keys, peer_u, peer_v, g_final):
    assert w_mod.shape[0] == 1 and x.shape[0] == 1, "one layer, one batch element"
    _, t, d = x.shape
    tc = ctx.shape[1]
    _, mh, dh = ml_norm_g.shape
    _, gh, dk = gd_norm_g.shape
    mlw, gdw = mh * dh, gh * dk
    _, ph, _, nk, half = peer_keys.shape
    ne = peer_u.shape[1]
    w = GRID_W
    r = t // w
    assert r == tc and t % CHUNK == 0 and tc % CHUNK == 0 and r % CHUNK == 0
    assert ne == nk * nk and nk & (nk - 1) == 0 and 4 * mh + 4 * gh <= LANES
    n_lat, n_ctx = t // CHUNK, tc // CHUNK

    x2 = x.reshape(t, d)
    ctx2 = ctx.reshape(tc, d)

    s8 = jnp.zeros((8, d), f32).at[0].set(c[0]).at[1].set(c_ctx)
    mod = _mod_vectors(s8, w_mod[0], b_mod)

    h_rm = _normmod(x2, ctx2, g_mix, mod, 0, 1, tm=tc)
    h_cm = _normmod_colmajor(x2.reshape(r, w, d), g_mix[0:1], mod, 0, 1, rb=_pick(r, 64)).reshape(t, d)

    main = 4 * mlw + 4 * gdw
    w_in0 = w_in[0].astype(bf16)
    ng = w_in0.shape[1] - main
    w_gate = jnp.pad(w_in0[:, main:], ((0, 0), (0, LANES - ng)))
    b_gate_row = jnp.pad(b_gate, ((0, 0), (0, LANES - ng)))
    rows = t + tc
    tm_all = _pick(rows, 1280)
    tm_lat = _pick(t, 1024)
    tn_in = _pick(math.gcd(4 * mlw, 4 * gdw), 512)
    gd_off = 4 * mlw // tn_in
    pm = _matmul(h_rm, w_in0, m=rows, tm=tm_all, tn=tn_in, n=4 * mlw, name="proj_ml")
    pg_lat = _matmul(h_cm, w_in0, m=t, tm=tm_lat, tn=tn_in, n=4 * gdw, b_off=gd_off, name="proj_gd_lat")
    pg_ctx = _matmul(h_rm, w_in0, m=tc, tm=tc, tn=tn_in, n=4 * gdw, b_off=gd_off, a_off=t // tc,
                     name="proj_gd_ctx")
    bias_spec = pl.BlockSpec((1, LANES), lambda i, j: (0, 0))
    add_bias = lambda acc, b_ref: acc + b_ref[...]
    g_rm = _matmul(h_rm, w_gate, m=rows, tm=tm_all, tn=LANES, extras=[(b_gate_row, bias_spec)],
                   epilogue=add_bias, name="proj_gate_rm")
    g_cm = _matmul(h_cm, w_gate, m=t, tm=tm_lat, tn=LANES, extras=[(b_gate_row, bias_spec)],
                   epilogue=add_bias, name="proj_gate_cm")

    o_a = 4 * mh + 2 * gh
    alog_row = jnp.zeros((1, LANES), f32).at[0, o_a:o_a + 2 * gh].set(gd_a_log.reshape(-1))
    ml_col, ml_row = _gate_prep(g_rm, n_lat, g_rm, n_lat, n_ctx, alog_row, mh, gh)
    gd_col, gd_row = _gate_prep(g_cm, n_lat, g_rm, n_lat, n_ctx, alog_row, mh, gh)

    hm_f, hm_b = _mlstm_scan(pm, ml_col, ml_row, n_lat, n_ctx, mh, dh)

    qkv = _gdn_conv(pg_lat, pg_ctx, gd_conv_w[0], gd_conv_b, gdw, dk)
    hg_f, hg_b = _gdn_scan(qkv, gd_col, gd_row, n_lat, n_ctx, mh, gh, dk)

    mix_ml = _mix_ml(hm_f, hm_b, pm, ml_norm_g.reshape(1, mlw), t, mh, dh, tm=_pick(t, 256))
    ncols = rows // r
    mix_gd = _mix_gd(hg_f.reshape(ncols, r, gdw), hg_b.reshape(ncols, r, gdw),
                     pg_lat.reshape(w, r, 4 * gdw), gd_norm_g.reshape(1, gdw), r, w, gh, dk,
                     rb=_pick(r, 32)).reshape(t, gdw)
    w_out0 = w_out[0].astype(bf16)
    x1 = _outproj(mix_ml, mix_gd, w_out0[:mlw], w_out0[mlw:], x2, mod, 2, tm=tm_lat, tn=512)

    f_in = _normmod(x1, None, g_ffn, mod, 3, 4, tm=_pick(t, 256))
    qp = _matmul(f_in, peer_wq[0].astype(bf16), m=t, tm=tm_lat, tn=512, name="peer_query")
    e_idx, gates = _peer_topk(qp, peer_keys[0].reshape(2 * ph, nk, half), ph, nk, half, tb=_pick(t, 256))
    w3 = _peer_weights(e_idx, gates, nk, tb=_pick(t, 128))
    act = _peer_act(f_in, peer_u[0].astype(bf16), w3, nk, tm=_pick(t, 1024))
    x2_out = _peer_out(act, peer_v[0].astype(bf16), tm=_pick(t, 1024), tn=2048, tk=1024)

    out = _final(x1, x2_out, mod, 5, g_final.reshape(1, d), tm=_pick(t, 256))
    aux = dict(mod=mod, h_rm=h_rm, h_cm=h_cm, pm=pm, pg_lat=pg_lat, pg_ctx=pg_ctx, g_rm=g_rm, g_cm=g_cm,
               ml_col=ml_col, ml_row=ml_row, gd_col=gd_col, gd_row=gd_row, hm_f=hm_f, hm_b=hm_b, qkv=qkv,
               hg_f=hg_f, hg_b=hg_b, mix_ml=mix_ml, mix_gd=mix_gd, x1=x1, f_in=f_in, qp=qp, e_idx=e_idx,
               gates=gates, w3=w3, act=act, x2_out=x2_out)
    return out.reshape(1, t, d), aux
```

```python
import functools
import math

import jax
import jax.numpy as jnp
from jax import lax
from jax.experimental import pallas as pl
from jax.experimental.pallas import tpu as pltpu

f32 = jnp.float32
bf16 = jnp.bfloat16
i32 = jnp.int32

GRID_W = 64
PEER_TOPK = 16
EPS = 1e-6
NEG = -1e30
CHUNK = 128
LANES = 128
VMEM_CAP = 56 * 1024 * 1024


def _params(block_bytes, scratch_bytes=0, semantics=None):
    est = 2 * block_bytes + scratch_bytes + 16 * 1024 * 1024
    return pltpu.CompilerParams(
        dimension_semantics=semantics,
        vmem_limit_bytes=int(min(max(est, 32 * 1024 * 1024), VMEM_CAP)))


def _bdot(a, b):
    return jnp.dot(a.astype(bf16), b.astype(bf16), preferred_element_type=f32)


def _bdot_nt(a, b):
    return lax.dot_general(a.astype(bf16), b.astype(bf16), (((1,), (1,)), ((), ())),
                           preferred_element_type=f32)


def _bdot_tn(a, b):
    return lax.dot_general(a.astype(bf16), b.astype(bf16), (((0,), (0,)), ((), ())),
                           preferred_element_type=f32)


def _split3(x):
    hi = x.astype(bf16)
    r = x - hi.astype(f32)
    mid = r.astype(bf16)
    lo = (r - mid.astype(f32)).astype(bf16)
    return hi, mid, lo


def _softplus(x):
    return jnp.maximum(x, 0.0) + jnp.log1p(jnp.exp(-jnp.abs(x)))


def _sigmoid(x):
    return 1.0 / (1.0 + jnp.exp(-x))


def _silu(x):
    return x * _sigmoid(x)


def _mod_body(s_ref, w_ref, b_ref, o_ref):
    o_ref[...] = _bdot(_silu(s_ref[...]), w_ref[...]) + b_ref[...]


def _mod_vectors(s8, w_mod, b_mod):
    d, n = w_mod.shape
    tn = min(512, n)
    return pl.pallas_call(
        _mod_body,
        out_shape=jax.ShapeDtypeStruct((8, n), f32),
        grid=(n // tn,),
        in_specs=[pl.BlockSpec((8, d), lambda j: (0, 0)),
                  pl.BlockSpec((d, tn), lambda j: (0, j)),
                  pl.BlockSpec((1, tn), lambda j: (0, j))],
        out_specs=pl.BlockSpec((8, tn), lambda j: (0, j)),
        compiler_params=_params(d * tn * 4 + 8 * d * 4, semantics=("arbitrary",)),
        name="mod_vectors",
    )(s8, w_mod, b_mod)


def _norm_rows(x, g, shift, scale):
    y = x * lax.rsqrt(jnp.mean(x * x, axis=-1, keepdims=True) + EPS)
    return (y * g) * (1.0 + scale) + shift


def _normmod_body(x_ref, c_ref, g_ref, sh_ref, sc_ref, o_ref, *, n_lat):
    is_ctx = pl.program_id(0) >= n_lat
    x = jnp.where(is_ctx, c_ref[...], x_ref[...])
    sh = jnp.where(is_ctx, sh_ref[1:2, :], sh_ref[0:1, :])
    sc = jnp.where(is_ctx, sc_ref[1:2, :], sc_ref[0:1, :])
    o_ref[...] = _norm_rows(x, g_ref[...], sh, sc).astype(bf16)


def _normmod(x, ctx, g, mod, shift_blk, scale_blk, tm):
    t, d = x.shape
    n_lat = t // tm
    if ctx is None:
        ctx = x
        n_ctx = 0
    else:
        n_ctx = ctx.shape[0] // tm
    rows = (n_lat + n_ctx) * tm
    return pl.pallas_call(
        functools.partial(_normmod_body, n_lat=n_lat),
        out_shape=jax.ShapeDtypeStruct((rows, d), bf16),
        grid=(n_lat + n_ctx,),
        in_specs=[pl.BlockSpec((tm, d), lambda i: (jnp.minimum(i, n_lat - 1), 0)),
                  pl.BlockSpec((tm, d), lambda i: (jnp.maximum(i - n_lat, 0), 0)),
                  pl.BlockSpec((1, d), lambda i: (0, 0)),
                  pl.BlockSpec((8, d), lambda i: (0, shift_blk)),
                  pl.BlockSpec((8, d), lambda i: (0, scale_blk))],
        out_specs=pl.BlockSpec((tm, d), lambda i: (i, 0)),
        compiler_params=_params(3 * tm * d * 4, semantics=("arbitrary",)),
        name="normmod",
    )(x, ctx, g, mod, mod)


def _normmod_cm_body(x_ref, g_ref, sh_ref, sc_ref, o_ref, *, ncol):
    x = jnp.swapaxes(x_ref[...], 0, 1)
    o_ref[...] = _norm_rows(x, g_ref[...], sh_ref[0:1, :], sc_ref[0:1, :]).astype(bf16)


def _normmod_colmajor(x3, g, mod, shift_blk, scale_blk, rb):
    r, w, d = x3.shape
    ncol = 8
    return pl.pallas_call(
        functools.partial(_normmod_cm_body, ncol=ncol),
        out_shape=jax.ShapeDtypeStruct((w, r, d), bf16),
        grid=(w // ncol, r // rb),
        in_specs=[pl.BlockSpec((rb, ncol, d), lambda c, j: (j, c, 0)),
                  pl.BlockSpec((1, d), lambda c, j: (0, 0)),
                  pl.BlockSpec((8, d), lambda c, j: (0, shift_blk)),
                  pl.BlockSpec((8, d), lambda c, j: (0, scale_blk))],
        out_specs=pl.BlockSpec((ncol, rb, d), lambda c, j: (c, j, 0)),
        compiler_params=_params(2 * rb * ncol * d * 4, semantics=("arbitrary", "arbitrary")),
        name="normmod_colmajor",
    )(x3, g, mod, mod)


def _mm_body(a_ref, b_ref, *rest, nt, epilogue):
    acc = (_bdot_nt if nt else _bdot)(a_ref[...], b_ref[...])
    o_ref = rest[-1]
    o_ref[...] = epilogue(acc, *rest[:-1]).astype(o_ref.dtype)


def _matmul(a, b, *, m, tm, tn, nt=False, a_off=0, n=None, b_off=0, extras=(), epilogue=None,
            out_dtype=f32, name="matmul"):
    k = a.shape[1]
    if n is None:
        n = b.shape[0] if nt else b.shape[1]
    tn = min(tn, n)
    if epilogue is None:
        epilogue = lambda acc: acc
    b_spec = (pl.BlockSpec((tn, k), lambda i, j: (j, 0)) if nt
              else pl.BlockSpec((k, tn), lambda i, j: (0, j + b_off)))
    extra_bytes = sum(jnp.dtype(e.dtype).itemsize * math.prod(s.block_shape) for e, s in extras)
    return pl.pallas_call(
        functools.partial(_mm_body, nt=nt, epilogue=epilogue),
        out_shape=jax.ShapeDtypeStruct((m, n), out_dtype),
        grid=(m // tm, n // tn),
        in_specs=[pl.BlockSpec((tm, k), lambda i, j: (i + a_off, 0)), b_spec] + [s for _, s in extras],
        out_specs=pl.BlockSpec((tm, tn), lambda i, j: (i, j)),
        compiler_params=_params(tm * k * 2 + k * tn * 2 + tm * tn * 4 + extra_bytes,
                                semantics=("arbitrary", "arbitrary")),
        name=name,
    )(a, b, *[e for e, _ in extras])


def _gate_body(ga_ref, gb_ref, alog_ref, col_ref, row_ref, *, n_a, mh, gh):
    o_f, o_b, o_a = 2 * mh, 4 * mh, 4 * mh + 2 * gh
    is_b = pl.program_id(0) >= n_a
    g = jnp.where(is_b, gb_ref[...], ga_ref[...])
    n = g.shape[0]
    lane = lax.broadcasted_iota(i32, (n, LANES), 1)
    lf = -_softplus(-g)
    gl = -jnp.exp(alog_ref[...]) * _softplus(g)
    src = jnp.where(lane < o_b, lf, gl)
    rid = lax.broadcasted_iota(i32, (n, n), 0)
    cid = lax.broadcasted_iota(i32, (n, n), 1)
    tri_f = jnp.where(cid <= rid, 1.0, 0.0).astype(bf16)
    tri_b = jnp.where(cid >= rid, 1.0, 0.0).astype(bf16)
    parts = _split3(src)
    cf = sum(jnp.dot(tri_f, p, preferred_element_type=f32) for p in parts)
    cb = sum(jnp.dot(tri_b, p, preferred_element_type=f32) for p in parts)
    fwd = ((lane >= o_f) & (lane < o_f + mh)) | ((lane >= o_a) & (lane < o_a + gh))
    bwd = ((lane >= o_f + mh) & (lane < o_b)) | ((lane >= o_a + gh) & (lane < o_a + 2 * gh))
    beta = (lane >= o_b) & (lane < o_a)
    out = jnp.where(lane < o_f, g,
                    jnp.where(fwd, cf, jnp.where(bwd, cb, jnp.where(beta, _sigmoid(g), 0.0))))
    col_ref[...] = out
    row_ref[0] = out.T


def _gate_prep(g_a, n_a, g_b, b_off, n_b, alog_row, mh, gh):
    nc = n_a + n_b
    return pl.pallas_call(
        functools.partial(_gate_body, n_a=n_a, mh=mh, gh=gh),
        out_shape=(jax.ShapeDtypeStruct((nc * CHUNK, LANES), f32),
                   jax.ShapeDtypeStruct((nc, LANES, CHUNK), f32)),
        grid=(nc,),
        in_specs=[pl.BlockSpec((CHUNK, LANES), lambda i: (jnp.minimum(i, n_a - 1), 0)),
                  pl.BlockSpec((CHUNK, LANES), lambda i: (jnp.maximum(i - n_a, 0) + b_off, 0)),
                  pl.BlockSpec((1, LANES), lambda i: (0, 0))],
        out_specs=(pl.BlockSpec((CHUNK, LANES), lambda i: (i, 0)),
                   pl.BlockSpec((1, LANES, CHUNK), lambda i: (i, 0, 0))),
        compiler_params=_params(4 * CHUNK * LANES * 4, semantics=("arbitrary",)),
        name="gate_prep",
    )(g_a, g_b, alog_row)


def _fwd_chunk(j, n_lat, n_ctx):
    return jnp.where(j < n_ctx, n_lat + j, j - n_ctx)


def _bwd_chunk(j, n_lat, n_ctx):
    return n_lat + n_ctx - 1 - j


def _tri_masks(n):
    rid = lax.broadcasted_iota(i32, (n, n), 0)
    cid = lax.broadcasted_iota(i32, (n, n), 1)
    return cid <= rid, cid >= rid, cid < rid, cid > rid


def _mlstm_body(qf_ref, kf_ref, vf_ref, cf_ref, rf_ref, qb_ref, kb_ref, vb_ref, cb_ref, rb_ref,
                hf_ref, hb_ref, c_scr, n_scr, m_scr, *, mh, dh):
    @pl.when(pl.program_id(0) == 0)
    def _():
        c_scr[...] = jnp.zeros_like(c_scr)
        n_scr[...] = jnp.zeros_like(n_scr)
        m_scr[...] = jnp.full_like(m_scr, NEG)

    n = CHUNK
    incl_f, incl_b, _, _ = _tri_masks(n)
    scale = dh ** -0.5
    dirs = ((0, qf_ref, kf_ref, vf_ref, cf_ref, rf_ref, hf_ref, incl_f),
            (1, qb_ref, kb_ref, vb_ref, cb_ref, rb_ref, hb_ref, incl_b))
    chains = []
    for d, q_ref, k_ref, v_ref, col_ref, row_ref, h_ref, incl in dirs:
        for h in range(mh):
            ch = d * mh + h
            li, lb = d * mh + h, 2 * mh + d * mh + h
            sl = slice(h * dh, (h + 1) * dh)
            brow = row_ref[0, lb:lb + 1, :]
            chains.append(dict(
                ch=ch, sl=sl, h_ref=h_ref, incl=incl,
                q=q_ref[:, sl] * scale, k=k_ref[:, sl], v=v_ref[:, sl],
                icol=col_ref[:, li:li + 1], bcol=col_ref[:, lb:lb + 1],
                irow=row_ref[0, li:li + 1, :], brow=brow,
                btot=brow[:, n - 1:n] if d == 0 else brow[:, 0:1],
                m_st=m_scr[ch, :, 0:1], c_st=c_scr[ch], n_st=n_scr[ch]))
    reps = dh // LANES
    ones = jnp.ones((n, LANES), bf16)
    wide = lambda x: jnp.concatenate([x] * reps, axis=1)
    for c in chains:
        bcol_bc = jnp.broadcast_to(c["bcol"], (n, LANES))
        dmat = jnp.where(c["incl"], bcol_bc - c["brow"] + c["irow"], NEG)
        m_row = jnp.maximum(c["bcol"] + c["m_st"], jnp.max(dmat, axis=1, keepdims=True))
        c["m_row"] = jnp.broadcast_to(m_row, (n, LANES))
        c["p"] = jnp.exp(dmat - c["m_row"])
        c["w_inter"] = jnp.exp(bcol_bc + c["m_st"] - c["m_row"])
    for c in chains:
        c["s"] = _bdot_nt(c["q"], c["k"]) * c["p"]
        c["qc"] = _bdot(c["q"], c["c_st"])
        c["qn"] = _bdot(c["q"], c["n_st"])
    for c in chains:
        s_hi = c["s"].astype(bf16)
        s_lo = (c["s"] - s_hi.astype(f32)).astype(bf16)
        row_sum = (jnp.dot(s_hi, ones, preferred_element_type=f32)
                   + jnp.dot(s_lo, ones, preferred_element_type=f32))
        den = c["w_inter"] * c["qn"] + row_sum
        inv = 1.0 / jnp.maximum(jnp.abs(den), jnp.exp(-c["m_row"]))
        num = wide(c["w_inter"]) * c["qc"] + _bdot(c["s"], c["v"])
        c["h_ref"][:, c["sl"]] = num * wide(inv)
    eye = jnp.where(lax.broadcasted_iota(i32, (dh, dh), 0) == lax.broadcasted_iota(i32, (dh, dh), 1),
                    1.0, 0.0).astype(bf16)
    for c in chains:
        ch = c["ch"]
        wlog = c["btot"] - c["bcol"] + c["icol"]
        m_new = jnp.maximum(c["btot"] + c["m_st"], jnp.max(wlog, axis=0, keepdims=True))
        a_st = jnp.exp(c["btot"] + c["m_st"] - m_new)
        kw = c["k"] * wide(jnp.broadcast_to(jnp.exp(wlog - m_new), (n, LANES)))
        kw_t = _bdot_nt(eye, kw)
        c_scr[ch] = a_st * c["c_st"] + _bdot(kw_t, c["v"])
        n_scr[ch] = a_st * c["n_st"] + _bdot(kw_t, ones)
        m_scr[ch] = jnp.broadcast_to(m_new, (1, LANES))


def _mlstm_scan(pm, gcol, grow, n_lat, n_ctx, mh, dh):
    rows = pm.shape[0]
    mlw = mh * dh
    nc = n_lat + n_ctx
    fc = functools.partial(_fwd_chunk, n_lat=n_lat, n_ctx=n_ctx)
    bc = functools.partial(_bwd_chunk, n_lat=n_lat, n_ctx=n_ctx)

    def specs(cm):
        return [pl.BlockSpec((CHUNK, mlw), lambda j, p=p: (cm(j), p)) for p in range(3)] + [
            pl.BlockSpec((CHUNK, LANES), lambda j: (cm(j), 0)),
            pl.BlockSpec((1, LANES, CHUNK), lambda j: (cm(j), 0, 0))]

    blk = 2 * (3 * CHUNK * mlw * 4 + 2 * CHUNK * LANES * 4 + CHUNK * mlw * 4)
    scr = 2 * mh * (dh * dh + dh * LANES + 8 * LANES) * 4
    return pl.pallas_call(
        functools.partial(_mlstm_body, mh=mh, dh=dh),
        out_shape=(jax.ShapeDtypeStruct((rows, mlw), f32), jax.ShapeDtypeStruct((rows, mlw), f32)),
        grid=(nc,),
        in_specs=specs(fc) + specs(bc),
        out_specs=(pl.BlockSpec((CHUNK, mlw), lambda j: (fc(j), 0)),
                   pl.BlockSpec((CHUNK, mlw), lambda j: (bc(j), 0))),
        scratch_shapes=[pltpu.VMEM((2 * mh, dh, dh), f32),
                        pltpu.VMEM((2 * mh, dh, LANES), f32),
                        pltpu.VMEM((2 * mh, 1, LANES), f32)],
        compiler_params=_params(blk, scr, semantics=("arbitrary",)),
        name="mlstm_scan",
    )(pm, pm, pm, gcol, grow, pm, pm, pm, gcol, grow)


def _gdn_conv_body(l_ref, c_ref, p_ref, n_ref, w_ref, b_ref, o_ref, ext, *, n_lat, kw, gdw, dk):
    b = pl.program_id(0)
    cw = o_ref.shape[1]
    rb = o_ref.shape[0]
    is_ctx = b >= n_lat
    half = kw // 2
    ext[0:8, :] = jnp.where((b == 0) | is_ctx, 0.0, p_ref[...])
    ext[8:8 + rb, :] = jnp.where(is_ctx, c_ref[...], l_ref[...])
    ext[8 + rb:16 + rb, :] = jnp.where(b >= n_lat - 1, 0.0, n_ref[...])
    y = jnp.broadcast_to(b_ref[...], (rb, cw))
    full = ext[...]
    for j in range(kw):
        y = y + pltpu.roll(full, (half - j) % (rb + 16), 0)[8:8 + rb, :] * w_ref[j:j + 1, :]
    y = _silu(y)
    part = (pl.program_id(1) * cw) // gdw
    qk_scale = jnp.where(part == 0, dk ** -0.5, 1.0)
    for hh in range(cw // dk):
        sl = slice(hh * dk, (hh + 1) * dk)
        t = y[:, sl]
        tn = t * lax.rsqrt(jnp.sum(t * t, axis=-1, keepdims=True) + EPS) * qk_scale
        o_ref[:, sl] = jnp.where(part == 2, t, tn)


def _gdn_conv(pg_lat, pg_ctx, conv_w, conv_b, gdw, dk):
    t = pg_lat.shape[0]
    rb = pg_ctx.shape[0]
    n_lat = t // rb
    kw = conv_w.shape[0]
    cw = gdw
    nh8 = rb // 8
    return pl.pallas_call(
        functools.partial(_gdn_conv_body, n_lat=n_lat, kw=kw, gdw=gdw, dk=dk),
        out_shape=jax.ShapeDtypeStruct((t + rb, 3 * gdw), f32),
        grid=(n_lat + 1, 3 * gdw // cw),
        in_specs=[pl.BlockSpec((rb, cw), lambda b, c: (jnp.minimum(b, n_lat - 1), c)),
                  pl.BlockSpec((rb, cw), lambda b, c: (0, c)),
                  pl.BlockSpec((8, cw), lambda b, c: (jnp.clip(b * nh8 - 1, 0, n_lat * nh8 - 1), c)),
                  pl.BlockSpec((8, cw), lambda b, c: (jnp.minimum((b + 1) * nh8, n_lat * nh8 - 1), c)),
                  pl.BlockSpec((kw, cw), lambda b, c: (0, c)),
                  pl.BlockSpec((1, cw), lambda b, c: (0, c))],
        out_specs=pl.BlockSpec((rb, cw), lambda b, c: (b, c)),
        scratch_shapes=[pltpu.VMEM((rb + 16, cw), f32)],
        compiler_params=_params(4 * rb * cw * 4, (rb + 16) * cw * 4, semantics=("arbitrary", "arbitrary")),
        name="gdn_conv",
    )(pg_lat, pg_ctx, pg_lat, pg_lat, conv_w, conv_b)


def _gdn_body(qf_ref, kf_ref, vf_ref, cf_ref, rf_ref, qb_ref, kb_ref, vb_ref, cb_ref, rb_ref,
              of_ref, ob_ref, s_scr, *, mh, gh, dk):
    @pl.when(pl.program_id(0) == 0)
    def _():
        s_scr[...] = jnp.zeros_like(s_scr)

    n = CHUNK
    incl_f, incl_b, strict_f, strict_b = _tri_masks(n)
    rid = lax.broadcasted_iota(i32, (n, n), 0)
    cid = lax.broadcasted_iota(i32, (n, n), 1)
    o_b, o_a = 4 * mh, 4 * mh + 2 * gh
    dirs = ((0, qf_ref, kf_ref, vf_ref, cf_ref, rf_ref, of_ref, incl_f, strict_f),
            (1, qb_ref, kb_ref, vb_ref, cb_ref, rb_ref, ob_ref, incl_b, strict_b))
    chains = []
    for d, q_ref, k_ref, v_ref, col_ref, row_ref, o_ref, incl, strict in dirs:
        for h in range(gh):
            lbeta, lg = o_b + d * gh + h, o_a + d * gh + h
            sl = slice(h * dk, (h + 1) * dk)
            grow = row_ref[0, lg:lg + 1, :]
            chains.append(dict(
                ch=d * gh + h, sl=sl, o_ref=o_ref, incl=incl, strict=strict,
                q=q_ref[:, sl], k=k_ref[:, sl], v=v_ref[:, sl],
                beta=col_ref[:, lbeta:lbeta + 1], gcol=col_ref[:, lg:lg + 1], grow=grow,
                glast=grow[:, n - 1:n] if d == 0 else grow[:, 0:1]))
    for c in chains:
        c["decay"] = jnp.exp(jnp.where(c["incl"], c["gcol"] - c["grow"], NEG))
        c["kb"] = c["k"] * c["beta"]
        c["egc"] = jnp.exp(c["gcol"])
    for c in chains:
        c["a"] = jnp.where(c["strict"], _bdot_nt(c["kb"], c["k"]) * c["decay"], 0.0)
        c["attn"] = _bdot_nt(c["q"], c["k"]) * c["decay"]
    eye = jnp.where(rid == cid, 1.0, 0.0)
    for c in chains:
        c["t"] = eye - jnp.where((rid >> 1) == (cid >> 1), c["a"], 0.0)
    sb = 1
    while (1 << sb) < n:
        off = ((rid >> (sb + 1)) == (cid >> (sb + 1))) & ((rid >> sb) != (cid >> sb))
        for c in chains:
            c["x"] = _bdot(jnp.where(off, c["a"], 0.0), c["t"])
        for c in chains:
            c["t"] = c["t"] - _bdot(c["t"], c["x"])
        sb += 1
    for c in chains:
        c["w"] = _bdot(c["t"], c["kb"] * c["egc"])
        c["u"] = _bdot(c["t"], c["v"] * c["beta"])
        c["s"] = s_scr[c["ch"]]
    for c in chains:
        c["v_new"] = c["u"] - _bdot(c["w"], c["s"])
        c["o"] = _bdot(c["q"] * c["egc"], c["s"])
    for c in chains:
        c["o_ref"][:, c["sl"]] = c["o"] + _bdot(c["attn"], c["v_new"])
        kd = c["k"] * jnp.exp(c["glast"] - c["gcol"])
        s_scr[c["ch"]] = jnp.exp(c["glast"]) * c["s"] + _bdot_tn(kd, c["v_new"])


def _gdn_scan(qkv, gcol, grow, n_lat, n_ctx, mh, gh, dk):
    rows = qkv.shape[0]
    gdw = gh * dk
    nc = n_lat + n_ctx
    fc = functools.partial(_fwd_chunk, n_lat=n_lat, n_ctx=n_ctx)
    bc = functools.partial(_bwd_chunk, n_lat=n_lat, n_ctx=n_ctx)

    def specs(cm):
        return [pl.BlockSpec((CHUNK, gdw), lambda j, p=p: (cm(j), p)) for p in range(3)] + [
            pl.BlockSpec((CHUNK, LANES), lambda j: (cm(j), 0)),
            pl.BlockSpec((1, LANES, CHUNK), lambda j: (cm(j), 0, 0))]

    blk = 2 * (3 * CHUNK * gdw * 4 + 2 * CHUNK * LANES * 4 + CHUNK * gdw * 4)
    return pl.pallas_call(
        functools.partial(_gdn_body, mh=mh, gh=gh, dk=dk),
        out_shape=(jax.ShapeDtypeStruct((rows, gdw), f32), jax.ShapeDtypeStruct((rows, gdw), f32)),
        grid=(nc,),
        in_specs=specs(fc) + specs(bc),
        out_specs=(pl.BlockSpec((CHUNK, gdw), lambda j: (fc(j), 0)),
                   pl.BlockSpec((CHUNK, gdw), lambda j: (bc(j), 0))),
        scratch_shapes=[pltpu.VMEM((2 * gh, dk, dk), f32)],
        compiler_params=_params(blk, 2 * gh * dk * dk * 4, semantics=("arbitrary",)),
        name="gdn_scan",
    )(qkv, qkv, qkv, gcol, grow, qkv, qkv, qkv, gcol, grow)


def _head_rms(t, g):
    return t * lax.rsqrt(jnp.mean(t * t, axis=-1, keepdims=True) + EPS) * g


def _mix_ml_body(hf_ref, hb_ref, o_gate_ref, g_ref, o_ref, *, mh, dh):
    for h in range(mh):
        sl = slice(h * dh, (h + 1) * dh)
        y = _head_rms(hf_ref[:, sl] + hb_ref[:, sl], g_ref[:, sl])
        o_ref[:, sl] = (_sigmoid(o_gate_ref[:, sl]) * y).astype(bf16)


def _mix_ml(hf, hb, pm, g_row, t, mh, dh, tm):
    mlw = mh * dh
    return pl.pallas_call(
        functools.partial(_mix_ml_body, mh=mh, dh=dh),
        out_shape=jax.ShapeDtypeStruct((t, mlw), bf16),
        grid=(t // tm,),
        in_specs=[pl.BlockSpec((tm, mlw), lambda i: (i, 0)),
                  pl.BlockSpec((tm, mlw), lambda i: (i, 0)),
                  pl.BlockSpec((tm, mlw), lambda i: (i, 3)),
                  pl.BlockSpec((1, mlw), lambda i: (0, 0))],
        out_specs=pl.BlockSpec((tm, mlw), lambda i: (i, 0)),
        compiler_params=_params(4 * tm * mlw * 4, semantics=("arbitrary",)),
        name="mix_ml",
    )(hf, hb, pm, g_row)


def _mix_gd_body(hf_ref, hb_ref, z_ref, g_ref, o_ref, *, gh, dk, ncol):
    for h in range(gh):
        sl = slice(h * dk, (h + 1) * dk)
        y = _head_rms(hf_ref[:, :, sl] + hb_ref[:, :, sl], g_ref[:, sl])
        o_ref[:, :, sl] = jnp.swapaxes(_silu(z_ref[:, :, sl]) * y, 0, 1).astype(bf16)


def _mix_gd(hf3, hb3, pg3, g_row, r, w, gh, dk, rb):
    gdw = gh * dk
    ncol = 16
    return pl.pallas_call(
        functools.partial(_mix_gd_body, gh=gh, dk=dk, ncol=ncol),
        out_shape=jax.ShapeDtypeStruct((r, w, gdw), bf16),
        grid=(w // ncol, r // rb),
        in_specs=[pl.BlockSpec((ncol, rb, gdw), lambda c, j: (c, j, 0)),
                  pl.BlockSpec((ncol, rb, gdw), lambda c, j: (c, j, 0)),
                  pl.BlockSpec((ncol, rb, gdw), lambda c, j: (c, j, 3)),
                  pl.BlockSpec((1, gdw), lambda c, j: (0, 0))],
        out_specs=pl.BlockSpec((rb, ncol, gdw), lambda c, j: (j, c, 0)),
        compiler_params=_params(4 * ncol * rb * gdw * 4, semantics=("arbitrary", "arbitrary")),
        name="mix_gd",
    )(hf3, hb3, pg3, g_row)


def _outproj_body(a1_ref, a2_ref, b1_ref, b2_ref, x_ref, gate_ref, o_ref):
    y = _bdot(a1_ref[...], b1_ref[...]) + _bdot(a2_ref[...], b2_ref[...])
    o_ref[...] = x_ref[...] + gate_ref[0:1, :] * y


def _outproj(a1, a2, w1, w2, x, mod, gate_blk, tm, tn):
    t, k1 = a1.shape
    k2 = a2.shape[1]
    d = x.shape[1]
    tn = min(tn, d)
    gb = gate_blk * (d // tn)
    return pl.pallas_call(
        _outproj_body,
        out_shape=jax.ShapeDtypeStruct((t, d), f32),
        grid=(t // tm, d // tn),
        in_specs=[pl.BlockSpec((tm, k1), lambda i, j: (i, 0)),
                  pl.BlockSpec((tm, k2), lambda i, j: (i, 0)),
                  pl.BlockSpec((k1, tn), lambda i, j: (0, j)),
                  pl.BlockSpec((k2, tn), lambda i, j: (0, j)),
                  pl.BlockSpec((tm, tn), lambda i, j: (i, j)),
                  pl.BlockSpec((8, tn), lambda i, j: (0, gb + j))],
        out_specs=pl.BlockSpec((tm, tn), lambda i, j: (i, j)),
        compiler_params=_params(tm * (k1 + k2) * 2 + (k1 + k2) * tn * 2 + 2 * tm * tn * 4,
                                semantics=("arbitrary", "arbitrary")),
        name="outproj",
    )(a1, a2, w1, w2, x, mod)


def _topk_rows(s, kk, n_rows):
    rid = lax.broadcasted_iota(i32, s.shape, 0)
    vals, idxs = [], []
    for _ in range(kk):
        m = jnp.max(s, axis=0, keepdims=True)
        idx = jnp.min(jnp.where(s == m, rid, n_rows), axis=0, keepdims=True)
        s = jnp.where(rid == idx, -jnp.inf, s)
        vals.append(m)
        idxs.append(idx)
    return vals, idxs


def _peer_topk_body(q_ref, keys_ref, e_ref, g_ref, *, ph, nk, half):
    kk = PEER_TOPK
    tb = q_ref.shape[0]
    e_rows, g_rows = [], []
    for h in range(ph):
        sv, si = [], []
        for p in range(2):
            grp = 2 * h + p
            qg = q_ref[:, grp * half:(grp + 1) * half]
            kg = keys_ref[grp]
            k_hi, k_mid, _ = _split3(kg)
            q_hi, q_mid, _ = _split3(qg)
            nt = lambda a, b: lax.dot_general(a, b, (((1,), (1,)), ((), ())), preferred_element_type=f32)
            s = nt(k_hi, q_hi) + (nt(k_hi, q_mid) + nt(k_mid, q_hi))
            vals, idxs = _topk_rows(s, kk, nk)
            sv.append(vals)
            si.append(idxs)
        pairs = [(a, b) for a in range(kk) for b in range(kk) if (a + 1) * (b + 1) <= kk]
        npad = -len(pairs) % 8
        cand = jnp.concatenate([sv[0][a] + sv[1][b] for a, b in pairs]
                               + [jnp.full((npad, tb), -jnp.inf, f32)], axis=0)
        eidx = jnp.concatenate([si[0][a] * nk + si[1][b] for a, b in pairs]
                               + [jnp.full((npad, tb), -1, i32)], axis=0)
        n_cand = len(pairs) + npad
        rid = lax.broadcasted_iota(i32, cand.shape, 0)
        fv, fe = [], []
        for _ in range(kk):
            m = jnp.max(cand, axis=0, keepdims=True)
            pos = jnp.min(jnp.where(cand == m, rid, n_cand), axis=0, keepdims=True)
            hit = rid == pos
            fe.append(jnp.max(jnp.where(hit, eidx, -1), axis=0, keepdims=True))
            cand = jnp.where(hit, -jnp.inf, cand)
            fv.append(m)
        fv = jnp.concatenate(fv, axis=0)
        ex = jnp.exp(fv - fv[0:1, :])
        g_rows.append(ex / jnp.sum(ex, axis=0, keepdims=True))
        e_rows.append(jnp.concatenate(fe, axis=0))
    e_ref[...] = jnp.concatenate(e_rows, axis=0).T
    g_ref[...] = jnp.concatenate(g_rows, axis=0).T


def _peer_topk(qp, keys, ph, nk, half, tb):
    t = qp.shape[0]
    hk = ph * PEER_TOPK
    return pl.pallas_call(
        functools.partial(_peer_topk_body, ph=ph, nk=nk, half=half),
        out_shape=(jax.ShapeDtypeStruct((t, hk), i32), jax.ShapeDtypeStruct((t, hk), f32)),
        grid=(t // tb,),
        in_specs=[pl.BlockSpec((tb, 2 * ph * half), lambda i: (i, 0)),
                  pl.BlockSpec((2 * ph, nk, half), lambda i: (0, 0, 0))],
        out_specs=(pl.BlockSpec((tb, hk), lambda i: (i, 0)), pl.BlockSpec((tb, hk), lambda i: (i, 0))),
        compiler_params=_params(tb * 2 * ph * half * 4 + 2 * ph * nk * half * 4 + 2 * tb * hk * 4,
                                semantics=("arbitrary",)),
        name="peer_topk",
    )(qp, keys)


def _peer_w_body(e_ref, g_ref, w_ref, *, nk):
    tb, hk = e_ref.shape
    shift = nk.bit_length() - 1
    sub = lax.broadcasted_iota(i32, (nk, hk), 0)

    def group(t8, carry):
        base = pl.multiple_of(t8 * 8, 8)
        e8 = e_ref[pl.ds(base, 8), :]
        g8 = g_ref[pl.ds(base, 8), :]
        ws = []
        for t in range(8):
            e = e8[t:t + 1, :]
            g = g8[t:t + 1, :]
            ei = lax.shift_right_logical(e, shift)
            ej = e & (nk - 1)
            g_hi = g.astype(bf16).astype(f32)
            pt = jnp.where(sub == ei, 1.0, 0.0).astype(bf16)
            hit = sub == ej
            qt = jnp.concatenate([jnp.where(hit, g_hi, 0.0).astype(bf16),
                                  jnp.where(hit, g - g_hi, 0.0).astype(bf16)], axis=1)
            ws.append(lax.dot_general(jnp.concatenate([pt, pt], axis=1), qt, (((1,), (1,)), ((), ())),
                                      preferred_element_type=f32))
        w_ref[:, pl.ds(base, 8), :] = jnp.swapaxes(jnp.stack(ws, axis=0), 0, 1)
        return carry

    lax.fori_loop(0, tb // 8, group, 0, unroll=4)


def _peer_weights(e, g, nk, tb):
    t, hk = e.shape
    return pl.pallas_call(
        functools.partial(_peer_w_body, nk=nk),
        out_shape=jax.ShapeDtypeStruct((nk, t, nk), f32),
        grid=(t // tb,),
        in_specs=[pl.BlockSpec((tb, hk), lambda i: (i, 0)), pl.BlockSpec((tb, hk), lambda i: (i, 0))],
        out_specs=pl.BlockSpec((nk, tb, nk), lambda i: (0, i, 0)),
        compiler_params=_params(tb * nk * nk * 4 + 2 * tb * hk * 4, semantics=("arbitrary",)),
        name="peer_weights",
    )(e, g)


def _gelu(x):
    return 0.5 * x * (1.0 + lax.erf(x * 0.7071067811865476))


def _peer_act_body(f_ref, u_ref, w_ref, o_ref, *, nk, gi):
    acc = _bdot_nt(f_ref[...], u_ref[...])
    for gg in range(gi):
        sl = slice(gg * nk, (gg + 1) * nk)
        o_ref[:, sl] = (w_ref[gg] * _gelu(acc[:, sl])).astype(bf16)


def _peer_act(f, u, w3, nk, tm):
    t, d = f.shape
    ne = u.shape[0]
    gi = 4
    tn = gi * nk
    return pl.pallas_call(
        functools.partial(_peer_act_body, nk=nk, gi=gi),
        out_shape=jax.ShapeDtypeStruct((t, ne), bf16),
        grid=(t // tm, ne // tn),
        in_specs=[pl.BlockSpec((tm, d), lambda i, j: (i, 0)),
                  pl.BlockSpec((tn, d), lambda i, j: (j, 0)),
                  pl.BlockSpec((gi, tm, nk), lambda i, j: (j, i, 0))],
        out_specs=pl.BlockSpec((tm, tn), lambda i, j: (i, j)),
        compiler_params=_params(tm * d * 2 + tn * d * 2 + tm * tn * 4 + tm * tn * 2 + tm * tn * 4,
                                semantics=("arbitrary", "arbitrary")),
        name="peer_act",
    )(f, u, w3)


def _peer_out_body(a_ref, v_ref, o_ref):
    @pl.when(pl.program_id(2) == 0)
    def _():
        o_ref[...] = jnp.zeros_like(o_ref)

    o_ref[...] += jnp.dot(a_ref[...], v_ref[...], preferred_element_type=f32)


def _peer_out(a, v, tm, tn, tk):
    t, ne = a.shape
    d = v.shape[1]
    tn = min(tn, d)
    tk = min(tk, ne)
    return pl.pallas_call(
        _peer_out_body,
        out_shape=jax.ShapeDtypeStruct((t, d), f32),
        grid=(t // tm, d // tn, ne // tk),
        in_specs=[pl.BlockSpec((tm, tk), lambda i, j, k: (i, k)),
                  pl.BlockSpec((tk, tn), lambda i, j, k: (k, j))],
        out_specs=pl.BlockSpec((tm, tn), lambda i, j, k: (i, j)),
        compiler_params=_params(tm * tk * 2 + tk * tn * 2 + 2 * tm * tn * 4,
                                semantics=("arbitrary", "arbitrary", "arbitrary")),
        name="peer_out",
    )(a, v)


def _final_body(x_ref, p_ref, gate_ref, g_ref, o_ref):
    x = x_ref[...] + gate_ref[0:1, :] * p_ref[...]
    o_ref[...] = x * lax.rsqrt(jnp.mean(x * x, axis=-1, keepdims=True) + EPS) * g_ref[...]


def _final(x, p, mod, gate_blk, g, tm):
    t, d = x.shape
    return pl.pallas_call(
        _final_body,
        out_shape=jax.ShapeDtypeStruct((t, d), f32),
        grid=(t // tm,),
        in_specs=[pl.BlockSpec((tm, d), lambda i: (i, 0)), pl.BlockSpec((tm, d), lambda i: (i, 0)),
                  pl.BlockSpec((8, d), lambda i: (0, gate_blk)), pl.BlockSpec((1, d), lambda i: (0, 0))],
        out_specs=pl.BlockSpec((tm, d), lambda i: (i, 0)),
        compiler_params=_params(3 * tm * d * 4, semantics=("arbitrary",)),
        name="final_norm",
    )(x, p, mod, g)


def _pick(n, pref):
    t = min(pref, n)
    while n % t:
        t -= 8
    return t


def kernel(x, c, ctx, c_ctx, w_mod, b_mod, g_mix, w_in, b_gate, gd_conv_w, gd_conv_b, gd_a_log,
           ml_norm_g, gd_norm_g, w_out, g_ffn, peer_wq, peer_keys, peer_u, peer_v, g_final):
    return _block(x, c, ctx, c_ctx, w_mod, b_mod, g_mix, w_in, b_gate, gd_conv_w, gd_conv_b, gd_a_log,
                  ml_norm_g, gd_norm_g, w_out, g_ffn, peer_wq, peer_keys, peer_u, peer_v, g_final)[0]


def _block(x, c, ctx, c_ctx, w_mod, b_mod, g_mix, w_in, b_gate, gd_conv_w, gd_conv_b, gd_a_log,
           ml_norm_g, gd_norm_g, w_out, g_ffn, peer_wq, peer_keys, peer_u, peer_v, g_final):
    assert w_mod.shape[0] == 1 and x.shape[0] == 1, "one layer, one batch element"
    _, t, d = x.shape
    tc = ctx.shape[1]
    _, mh, dh = ml_norm_g.shape
    _, gh, dk = gd_norm_g.shape
    mlw, gdw = mh * dh, gh * dk
    _, ph, _, nk, half = peer_keys.shape
    ne = peer_u.shape[1]
    w = GRID_W
    r = t // w
    assert r == tc and t % CHUNK == 0 and tc % CHUNK == 0 and r % CHUNK == 0
    assert ne == nk * nk and nk & (nk - 1) == 0 and 4 * mh + 4 * gh <= LANES
    n_lat, n_ctx = t // CHUNK, tc // CHUNK

    x2 = x.reshape(t, d)
    ctx2 = ctx.reshape(tc, d)

    s8 = jnp.zeros((8, d), f32).at[0].set(c[0]).at[1].set(c_ctx)
    mod = _mod_vectors(s8, w_mod[0], b_mod)

    h_rm = _normmod(x2, ctx2, g_mix, mod, 0, 1, tm=tc)
    h_cm = _normmod_colmajor(x2.reshape(r, w, d), g_mix[0:1], mod, 0, 1, rb=_pick(r, 64)).reshape(t, d)

    main = 4 * mlw + 4 * gdw
    w_in0 = w_in[0].astype(bf16)
    ng = w_in0.shape[1] - main
    w_gate = jnp.pad(w_in0[:, main:], ((0, 0), (0, LANES - ng)))
    b_gate_row = jnp.pad(b_gate, ((0, 0), (0, LANES - ng)))
    rows = t + tc
    tm_all = _pick(rows, 1280)
    tm_lat = _pick(t, 1024)
    tn_in = _pick(math.gcd(4 * mlw, 4 * gdw), 512)
    gd_off = 4 * mlw // tn_in
    pm = _matmul(h_rm, w_in0, m=rows, tm=tm_all, tn=tn_in, n=4 * mlw, name="proj_ml")
    pg_lat = _matmul(h_cm, w_in0, m=t, tm=tm_lat, tn=tn_in, n=4 * gdw, b_off=gd_off, name="proj_gd_lat")
    pg_ctx = _matmul(h_rm, w_in0, m=tc, tm=tc, tn=tn_in, n=4 * gdw, b_off=gd_off, a_off=t // tc,
                     name="proj_gd_ctx")
    bias_spec = pl.BlockSpec((1, LANES), lambda i, j: (0, 0))
    add_bias = lambda acc, b_ref: acc + b_ref[...]
    g_rm = _matmul(h_rm, w_gate, m=rows, tm=tm_all, tn=LANES, extras=[(b_gate_row, bias_spec)],
                   epilogue=add_bias, name="proj_gate_rm")
    g_cm = _matmul(h_cm, w_gate, m=t, tm=tm_lat, tn=LANES, extras=[(b_gate_row, bias_spec)],
                   epilogue=add_bias, name="proj_gate_cm")

    o_a = 4 * mh + 2 * gh
    alog_row = jnp.zeros((1, LANES), f32).at[0, o_a:o_a + 2 * gh].set(gd_a_log.reshape(-1))
    ml_col, ml_row = _gate_prep(g_rm, n_lat, g_rm, n_lat, n_ctx, alog_row, mh, gh)
    gd_col, gd_row = _gate_prep(g_cm, n_lat, g_rm, n_lat, n_ctx, alog_row, mh, gh)

    hm_f, hm_b = _mlstm_scan(pm, ml_col, ml_row, n_lat, n_ctx, mh, dh)

    qkv = _gdn_conv(pg_lat, pg_ctx, gd_conv_w[0], gd_conv_b, gdw, dk)
    hg_f, hg_b = _gdn_scan(qkv, gd_col, gd_row, n_lat, n_ctx, mh, gh, dk)

    mix_ml = _mix_ml(hm_f, hm_b, pm, ml_norm_g.reshape(1, mlw), t, mh, dh, tm=_pick(t, 256))
    ncols = rows // r
    mix_gd = _mix_gd(hg_f.reshape(ncols, r, gdw), hg_b.reshape(ncols, r, gdw),
                     pg_lat.reshape(w, r, 4 * gdw), gd_norm_g.reshape(1, gdw), r, w, gh, dk,
                     rb=_pick(r, 32)).reshape(t, gdw)
    w_out0 = w_out[0].astype(bf16)
    x1 = _outproj(mix_ml, mix_gd, w_out0[:mlw], w_out0[mlw:], x2, mod, 2, tm=tm_lat, tn=512)

    f_in = _normmod(x1, None, g_ffn, mod, 3, 4, tm=_pick(t, 256))
    qp = _matmul(f_in, peer_wq[0].astype(bf16), m=t, tm=tm_lat, tn=512, name="peer_query")
    e_idx, gates = _peer_topk(qp, peer_keys[0].reshape(2 * ph, nk, half), ph, nk, half, tb=_pick(t, 256))
    w3 = _peer_weights(e_idx, gates, nk, tb=_pick(t, 128))
    act = _peer_act(f_in, peer_u[0].astype(bf16), w3, nk, tm=_pick(t, 1024))
    x2_out = _peer_out(act, peer_v[0].astype(bf16), tm=_pick(t, 1024), tn=2048, tk=1024)

    out = _final(x1, x2_out, mod, 5, g_final.reshape(1, d), tm=_pick(t, 256))
    aux = dict(mod=mod, h_rm=h_rm, h_cm=h_cm, pm=pm, pg_lat=pg_lat, pg_ctx=pg_ctx, g_rm=g_rm, g_cm=g_cm,
               ml_col=ml_col, ml_row=ml_row, gd_col=gd_col, gd_row=gd_row, hm_f=hm_f, hm_b=hm_b, qkv=qkv,
               hg_f=hg_f, hg_b=hg_b, mix_ml=mix_ml, mix_gd=mix_gd, x1=x1, f_in=f_in, qp=qp, e_idx=e_idx,
               gates=gates, w3=w3, act=act, x2_out=x2_out)
    return out.reshape(1, t, d), aux
```
